```python
import jax, jax.numpy as jnp
from jax import lax
import numpy as np

D_MODEL = 2048
BATCH = 2
SEQ = 8192
DEPTH = 2
DEC_BATCH = 16
DEC_SEQ = 16
PAST_LEN = 1024

CHUNK = 64
N_MIXERS = 2
N_CONV_LAYERS = (DEPTH + 1) // 2
N_FOX_LAYERS = DEPTH // 2
HEAD_DIM = 128
MIX_WIDTH = D_MODEL
XATTN_HEADS = 4
XATTN_WIDTH = XATTN_HEADS * HEAD_DIM
MIXER_WIDTH = MIX_WIDTH - XATTN_WIDTH
CONV_CH = MIXER_WIDTH
CONV_WIDTH = 31
CONV_STATE = CONV_WIDTH - 1
FOX_HEADS = MIXER_WIDTH // HEAD_DIM
FOX_WIDTH = FOX_HEADS * HEAD_DIM
CONV_IN = 2 * CONV_CH + XATTN_WIDTH
FOX_IN = 3 * FOX_WIDTH + FOX_HEADS + XATTN_WIDTH
N_MEM = 256
D_FF = 4 * D_MODEL
Q_BLOCK = 128
EPS = 1e-6
ATTN_SCALE = HEAD_DIM ** -0.5

kernel_name = "hybrid_conv_fox_stream_step"


def rmsnorm(x, g):
    xf = x.astype(jnp.float32)
    y = xf * lax.rsqrt(jnp.mean(jnp.square(xf), axis=-1, keepdims=True) + EPS)
    return (y * g.astype(jnp.float32)).astype(x.dtype)


def layernorm(x, g, b):
    xf = x.astype(jnp.float32)
    xc = xf - jnp.mean(xf, axis=-1, keepdims=True)
    y = xc * lax.rsqrt(jnp.mean(jnp.square(xc), axis=-1, keepdims=True) + EPS)
    return (y * g.astype(jnp.float32) + b.astype(jnp.float32)).astype(x.dtype)


def causal_dwconv(u_hist, w, b):
    out = lax.conv_general_dilated(u_hist, w[:, None, :].astype(u_hist.dtype), window_strides=(1,),
                                   padding='VALID', dimension_numbers=('NWC', 'WIO', 'NWC'),
                                   feature_group_count=u_hist.shape[-1])
    return out + b.astype(out.dtype)


def conformer_conv(z, u_prev, conv_w, conv_b, ln_g, ln_b):
    a, gate = jnp.split(z, 2, axis=-1)
    u = a * jax.nn.sigmoid(gate)
    u_hist = jnp.concatenate([u_prev.astype(u.dtype), u], axis=1)
    c = layernorm(causal_dwconv(u_hist, conv_w, conv_b), ln_g, ln_b)
    return jax.nn.silu(c), u_hist[:, u_hist.shape[1] - CONV_STATE:]


def fox_project(h, w_in, b_f):
    bsz, t = h.shape[0], h.shape[1]
    z = h @ w_in
    q = z[..., :FOX_WIDTH].reshape(bsz, t, FOX_HEADS, HEAD_DIM)
    k = z[..., FOX_WIDTH:2 * FOX_WIDTH].reshape(bsz, t, FOX_HEADS, HEAD_DIM)
    v = z[..., 2 * FOX_WIDTH:3 * FOX_WIDTH].reshape(bsz, t, FOX_HEADS, HEAD_DIM)
    f = z[..., 3 * FOX_WIDTH:3 * FOX_WIDTH + FOX_HEADS]
    qx = z[..., 3 * FOX_WIDTH + FOX_HEADS:].reshape(bsz, t, XATTN_HEADS, HEAD_DIM)
    logf = jax.nn.log_sigmoid(f.astype(jnp.float32) + b_f.astype(jnp.float32))
    return q, k, v, logf, qx


def fox_attend(q, k, v, cq, ck, q_pos, k_pos):
    s = jnp.einsum('bqhd,bkhd->bhqk', q, k, preferred_element_type=jnp.float32) * ATTN_SCALE
    bias = jnp.swapaxes(cq, 1, 2)[:, :, :, None] - jnp.swapaxes(ck, 1, 2)[:, :, None, :]
    mask = (k_pos[None, :] <= q_pos[:, None])[None, None]
    s = jnp.where(mask, s + bias, -jnp.inf)
    p = jax.nn.softmax(s, axis=-1)
    return jnp.einsum('bhqk,bkhd->bqhd', p.astype(v.dtype), v)


def fox_prompt_attention(q, k, v, logf):
    bsz, s_len = q.shape[0], q.shape[1]
    n_blk = s_len // Q_BLOCK
    c = jnp.cumsum(logf, axis=1)
    qb = q.reshape(bsz, n_blk, Q_BLOCK, FOX_HEADS, HEAD_DIM).transpose(1, 0, 2, 3, 4)
    cb = c.reshape(bsz, n_blk, Q_BLOCK, FOX_HEADS).transpose(1, 0, 2, 3)
    k_pos = jnp.arange(s_len)

    def one_block(args):
        i, qi, ci = args
        return fox_attend(qi, k, v, ci, c, i * Q_BLOCK + jnp.arange(Q_BLOCK), k_pos)

    ob = lax.map(one_block, (jnp.arange(n_blk), qb, cb))
    return ob.transpose(1, 0, 2, 3, 4).reshape(bsz, s_len, FOX_WIDTH)


def fox_sample_attention(q, k, v, logf, cache_k, cache_v, cache_logf):
    bsz, t = q.shape[0], q.shape[1]
    past = cache_k.shape[1]
    k_all = jnp.concatenate([cache_k.astype(k.dtype), k], axis=1)
    v_all = jnp.concatenate([cache_v.astype(v.dtype), v], axis=1)
    c_all = jnp.cumsum(jnp.concatenate([cache_logf.astype(jnp.float32), logf], axis=1), axis=1)
    o = fox_attend(q, k_all, v_all, c_all[:, past:], c_all, past + jnp.arange(t), jnp.arange(past + t))
    return o.reshape(bsz, t, FOX_WIDTH)


def memory_kv(mem, g, w_k, w_v):
    bsz = mem.shape[0]
    m = rmsnorm(mem, g)
    mk = (m @ w_k).reshape(bsz, N_MEM, XATTN_HEADS, HEAD_DIM)
    mv = (m @ w_v).reshape(bsz, N_MEM, XATTN_HEADS, HEAD_DIM)
    return mk, mv


def mem_attend(qx, mk, mv):
    bsz, t = qx.shape[0], qx.shape[1]
    s = jnp.einsum('bqhd,bmhd->bhqm', qx, mk.astype(qx.dtype), preferred_element_type=jnp.float32) * ATTN_SCALE
    p = jax.nn.softmax(s, axis=-1)
    o = jnp.einsum('bhqm,bmhd->bqhd', p.astype(qx.dtype), mv.astype(qx.dtype))
    return o.reshape(bsz, t, XATTN_WIDTH)


def sq_relu_mlp(x, g, w_up, w_down):
    h = rmsnorm(x, g) @ w_up
    return jnp.square(jax.nn.relu(h)) @ w_down


def setup_inputs(seed: int = 0) -> dict:
    key = jax.random.key(seed)
    ks = jax.random.split(key, 26)
    d = D_MODEL

    def nrm(k, shape, scale=1.0):
        return scale * jax.random.normal(k, shape, jnp.float32)

    return {
        'x_prompt': nrm(ks[0], (BATCH, SEQ, d)),
        'x_sample': nrm(ks[1], (DEC_BATCH, DEC_SEQ, d)),
        'cache_mem_k': nrm(ks[2], (DEPTH, DEC_BATCH, N_MEM, XATTN_HEADS, HEAD_DIM)),
        'cache_mem_v': nrm(ks[3], (DEPTH, DEC_BATCH, N_MEM, XATTN_HEADS, HEAD_DIM)),
        'state_conv': nrm(ks[4], (N_CONV_LAYERS, DEC_BATCH, CONV_STATE, CONV_CH), 0.5),
        'cache_fox_k': nrm(ks[5], (N_FOX_LAYERS, DEC_BATCH, PAST_LEN, FOX_HEADS, HEAD_DIM)),
        'cache_fox_v': nrm(ks[6], (N_FOX_LAYERS, DEC_BATCH, PAST_LEN, FOX_HEADS, HEAD_DIM)),
        'cache_fox_logf': jax.nn.log_sigmoid(nrm(ks[7], (N_FOX_LAYERS, DEC_BATCH, PAST_LEN, FOX_HEADS)) + 2.5),
        'mem_prompt': nrm(ks[8], (BATCH, N_MEM, d)),
        'g_mix': 1.0 + nrm(ks[9], (DEPTH, d), 0.02),
        'g_mem': 1.0 + nrm(ks[10], (DEPTH, d), 0.02),
        'w_mem_k': nrm(ks[11], (DEPTH, d, XATTN_WIDTH), d ** -0.5),
        'w_mem_v': nrm(ks[12], (DEPTH, d, XATTN_WIDTH), d ** -0.5),
        'w_in_conv': nrm(ks[13], (N_CONV_LAYERS, d, CONV_IN), d ** -0.5),
        'conv_w': nrm(ks[14], (N_CONV_LAYERS, CONV_WIDTH, CONV_CH), CONV_WIDTH ** -0.5),
        'conv_b': nrm(ks[15], (N_CONV_LAYERS, CONV_CH), 0.02),
        'conv_ln_g': 1.0 + nrm(ks[16], (N_CONV_LAYERS, CONV_CH), 0.02),
        'conv_ln_b': nrm(ks[17], (N_CONV_LAYERS, CONV_CH), 0.02),
        'w_in_fox': nrm(ks[18], (N_FOX_LAYERS, d, FOX_IN), d ** -0.5),
        'b_fox_f': jax.random.uniform(ks[19], (N_FOX_LAYERS, FOX_HEADS), jnp.float32, 1.0, 4.0),
        'w_out': nrm(ks[20], (DEPTH, MIX_WIDTH, d), MIX_WIDTH ** -0.5),
        'g_mlp': 1.0 + nrm(ks[21], (DEPTH, d), 0.02),
        'w_up': nrm(ks[22], (DEPTH, d, D_FF), d ** -0.5),
        'w_down': nrm(ks[23], (DEPTH, D_FF, d), D_FF ** -0.5),
        'g_final': 1.0 + nrm(ks[24], (d,), 0.02),
    }


def reference(x_prompt, x_sample, cache_mem_k, cache_mem_v, state_conv, cache_fox_k, cache_fox_v, cache_fox_logf,
              mem_prompt, g_mix, g_mem, w_mem_k, w_mem_v, w_in_conv, conv_w, conv_b, conv_ln_g, conv_ln_b,
              w_in_fox, b_fox_f, w_out, g_mlp, w_up, w_down, g_final):
    yp, ys = x_prompt, x_sample
    bp, sp = x_prompt.shape[0], x_prompt.shape[1]
    bs, ss = x_sample.shape[0], x_sample.shape[1]
    mem_k_p, mem_v_p, conv_p, conv_s = [], [], [], []
    fk_p, fv_p, fl_p, fk_s, fv_s, fl_s = [], [], [], [], [], []
    for i in range(DEPTH):
        j = i // N_MIXERS
        mk_p, mv_p = memory_kv(mem_prompt, g_mem[i], w_mem_k[i], w_mem_v[i])
        mem_k_p.append(mk_p)
        mem_v_p.append(mv_p)
        hp = rmsnorm(yp, g_mix[i])
        hs = rmsnorm(ys, g_mix[i])
        if i % N_MIXERS == 0:
            zp = hp @ w_in_conv[j]
            zs = hs @ w_in_conv[j]
            u0 = jnp.zeros((bp, CONV_STATE, CONV_CH), dtype=zp.dtype)
            mix_p, st_p = conformer_conv(zp[..., :2 * CONV_CH], u0, conv_w[j], conv_b[j], conv_ln_g[j], conv_ln_b[j])
            mix_s, st_s = conformer_conv(zs[..., :2 * CONV_CH], state_conv[j], conv_w[j], conv_b[j],
                                         conv_ln_g[j], conv_ln_b[j])
            conv_p.append(st_p)
            conv_s.append(st_s)
            qx_p = zp[..., 2 * CONV_CH:].reshape(bp, sp, XATTN_HEADS, HEAD_DIM)
            qx_s = zs[..., 2 * CONV_CH:].reshape(bs, ss, XATTN_HEADS, HEAD_DIM)
        else:
            q, k, v, lf, qx_p = fox_project(hp, w_in_fox[j], b_fox_f[j])
            mix_p = fox_prompt_attention(q, k, v, lf)
            fk_p.append(k)
            fv_p.append(v)
            fl_p.append(lf)
            q, k, v, lf, qx_s = fox_project(hs, w_in_fox[j], b_fox_f[j])
            mix_s = fox_sample_attention(q, k, v, lf, cache_fox_k[j], cache_fox_v[j], cache_fox_logf[j])
            fk_s.append(k)
            fv_s.append(v)
            fl_s.append(lf)
        xo_p = mem_attend(qx_p, mk_p, mv_p)
        xo_s = mem_attend(qx_s, cache_mem_k[i], cache_mem_v[i])
        yp = yp + jnp.concatenate([mix_p, xo_p.astype(mix_p.dtype)], axis=-1) @ w_out[i]
        ys = ys + jnp.concatenate([mix_s, xo_s.astype(mix_s.dtype)], axis=-1) @ w_out[i]
        yp = yp + sq_relu_mlp(yp, g_mlp[i], w_up[i], w_down[i])
        ys = ys + sq_relu_mlp(ys, g_mlp[i], w_up[i], w_down[i])
    y_prompt = rmsnorm(yp, g_final)
    y_sample = rmsnorm(ys, g_final)
    return (y_prompt, y_sample, jnp.stack(mem_k_p), jnp.stack(mem_v_p), jnp.stack(conv_p), jnp.stack(conv_s),
            jnp.stack(fk_p), jnp.stack(fv_p), jnp.stack(fl_p), jnp.stack(fk_s), jnp.stack(fv_s), jnp.stack(fl_s))
```

```python
import functools

import jax
import jax.numpy as jnp
from jax import lax
from jax.experimental import pallas as pl
from jax.experimental.pallas import tpu as pltpu

F32 = jnp.float32
BF16 = jnp.bfloat16

D_MODEL = 2048
CHUNK_HEAD = 128
XATTN_HEADS = 4
XATTN_WIDTH = XATTN_HEADS * CHUNK_HEAD
MIXER_WIDTH = D_MODEL - XATTN_WIDTH
FOX_HEADS = MIXER_WIDTH // CHUNK_HEAD
CONV_WIDTH = 31
CONV_STATE = CONV_WIDTH - 1
CONV_HIST = 32
D_FF = 4 * D_MODEL
EPS = 1e-6
ATTN_SCALE = CHUNK_HEAD ** -0.5
LOGF_ROWS = 16
LANE = 128

VMEM_LIMIT = 56 * 1024 * 1024


def _params(sem):
    return pltpu.CompilerParams(dimension_semantics=sem, vmem_limit_bytes=VMEM_LIMIT)


def _rms(x, g):
    ms = jnp.mean(x * x, axis=-1, keepdims=True)
    return x * lax.rsqrt(ms + EPS) * g


def _sigmoid(x):
    return 1.0 / (1.0 + jnp.exp(-x))


def _dot(a, b):
    return jnp.dot(a, b, preferred_element_type=F32)


def _dot_nt(a, b):
    return lax.dot_general(a, b, (((1,), (1,)), ((), ())), preferred_element_type=F32)


def _rms_matmul_kernel(x_ref, g_ref, w_ref, o_ref):
    xn = _rms(x_ref[...], g_ref[...]).astype(BF16)
    o_ref[...] = _dot(xn, w_ref[...])


def rms_matmul(x, g, w, tn):
    t, d = x.shape
    n = w.shape[1]
    return pl.pallas_call(
        _rms_matmul_kernel,
        grid=(n // tn,),
        in_specs=[pl.BlockSpec((t, d), lambda j: (0, 0)),
                  pl.BlockSpec((1, d), lambda j: (0, 0)),
                  pl.BlockSpec((d, tn), lambda j: (0, j))],
        out_specs=pl.BlockSpec((t, tn), lambda j: (0, j)),
        out_shape=jax.ShapeDtypeStruct((t, n), F32),
        compiler_params=_params(("arbitrary",)),
        name="rms_matmul",
    )(x, g, w)


def _conv_inproj_kernel(x_ref, g_ref, wa_ref, wg_ref, wq_ref, u_ref, qx_ref, xn_ref, *, n_glu):
    j = pl.program_id(1)

    @pl.when(j == 0)
    def _():
        xn_ref[...] = _rms(x_ref[...], g_ref[...]).astype(BF16)

    @pl.when(j < n_glu)
    def _():
        xn = xn_ref[...]
        a = _dot(xn, wa_ref[...])
        gate = _dot(xn, wg_ref[...])
        u_ref[...] = a * _sigmoid(gate)

    @pl.when(j == n_glu)
    def _():
        qx_ref[...] = _dot(xn_ref[...], wq_ref[...]).astype(BF16)


def conv_inproj(x, g, w, tm, tn):
    t, d = x.shape
    c = MIXER_WIDTH
    n_glu = c // tn
    last = n_glu - 1
    return pl.pallas_call(
        functools.partial(_conv_inproj_kernel, n_glu=n_glu),
        grid=(t // tm, n_glu + 1),
        in_specs=[pl.BlockSpec((tm, d), lambda i, j: (i, 0)),
                  pl.BlockSpec((1, d), lambda i, j: (0, 0)),
                  pl.BlockSpec((d, tn), lambda i, j: (0, jnp.minimum(j, last))),
                  pl.BlockSpec((d, tn), lambda i, j: (0, n_glu + jnp.minimum(j, last))),
                  pl.BlockSpec((d, XATTN_WIDTH), lambda i, j: (0, 2 * c // XATTN_WIDTH))],
        out_specs=[pl.BlockSpec((tm, tn), lambda i, j: (i, jnp.minimum(j, last))),
                   pl.BlockSpec((tm, XATTN_WIDTH), lambda i, j: (i, 0))],
        out_shape=[jax.ShapeDtypeStruct((t, c), F32),
                   jax.ShapeDtypeStruct((t, XATTN_WIDTH), BF16)],
        scratch_shapes=[pltpu.VMEM((tm, d), BF16)],
        compiler_params=_params(("arbitrary", "arbitrary")),
        name="conv_inproj",
    )(x, g, w, w, w)


def _conv_mixer_kernel(u_ref, hist_ref, w_ref, b_ref, lg_ref, lb_ref, o_ref, ubuf, cbuf, *, tt, tr, carry):
    i = pl.program_id(1)
    c = u_ref.shape[-1]

    @pl.when(i == 0)
    def _():
        ubuf[0:CONV_HIST, :] = hist_ref[0]

    ubuf[CONV_HIST:CONV_HIST + tt, :] = u_ref[0]

    off = CONV_HIST - CONV_STATE
    for c0 in range(0, c, LANE):
        for r0 in range(0, tt, tr):
            acc = jnp.zeros((tr, LANE), F32) + b_ref[:, c0:c0 + LANE]
            for k in range(CONV_WIDTH):
                acc = acc + w_ref[k:k + 1, c0:c0 + LANE] * ubuf[r0 + off + k:r0 + off + k + tr, c0:c0 + LANE]
            cbuf[r0:r0 + tr, c0:c0 + LANE] = acc

    y = cbuf[...]
    yc = y - jnp.mean(y, axis=-1, keepdims=True)
    yn = yc * lax.rsqrt(jnp.mean(yc * yc, axis=-1, keepdims=True) + EPS)
    yn = yn * lg_ref[...] + lb_ref[...]
    o_ref[0] = (yn * _sigmoid(yn)).astype(BF16)

    if carry:
        ubuf[0:CONV_HIST, :] = ubuf[tt:tt + CONV_HIST, :]


def conv_mixer(u, hist, conv_w, conv_b, ln_g, ln_b, tt):
    b, t, c = u.shape
    nblk = t // tt
    tr = min(tt, 64)
    return pl.pallas_call(
        functools.partial(_conv_mixer_kernel, tt=tt, tr=tr, carry=nblk > 1),
        grid=(b, nblk),
        in_specs=[pl.BlockSpec((1, tt, c), lambda bi, i: (bi, i, 0)),
                  pl.BlockSpec((1, CONV_HIST, c), lambda bi, i: (bi, 0, 0)),
                  pl.BlockSpec((CONV_WIDTH, c), lambda bi, i: (0, 0)),
                  pl.BlockSpec((1, c), lambda bi, i: (0, 0)),
                  pl.BlockSpec((1, c), lambda bi, i: (0, 0)),
                  pl.BlockSpec((1, c), lambda bi, i: (0, 0))],
        out_specs=pl.BlockSpec((1, tt, c), lambda bi, i: (bi, i, 0)),
        out_shape=jax.ShapeDtypeStruct((b, t, c), BF16),
        scratch_shapes=[pltpu.VMEM((CONV_HIST + tt, c), F32), pltpu.VMEM((tt, c), F32)],
        compiler_params=_params(("arbitrary", "arbitrary")),
        name="conv_mixer",
    )(u, hist, conv_w, conv_b, ln_g, ln_b)


def _fox_inproj_kernel(x_ref, g_ref, w_ref, wf_ref, bf_ref,
                       q_ref, kf_ref, kb_ref, vf_ref, vb_ref, qx_ref, lf_ref, xn_ref, *, n_part):
    j = pl.program_id(1)

    @pl.when(j == 0)
    def _():
        xn = _rms(x_ref[...], g_ref[...]).astype(BF16)
        xn_ref[...] = xn
        f = _dot_nt(wf_ref[...], xn) + bf_ref[...]
        lf_ref[...] = jnp.minimum(f, 0.0) - jnp.log1p(jnp.exp(-jnp.abs(f)))

    z = _dot(xn_ref[...], w_ref[...])

    @pl.when(j < n_part)
    def _():
        q_ref[...] = (z * ATTN_SCALE).astype(BF16)

    @pl.when(jnp.logical_and(j >= n_part, j < 2 * n_part))
    def _():
        kf_ref[...] = z
        kb_ref[...] = z.astype(BF16)

    @pl.when(jnp.logical_and(j >= 2 * n_part, j < 3 * n_part))
    def _():
        vf_ref[...] = z
        vb_ref[...] = z.astype(BF16)

    @pl.when(j == 3 * n_part)
    def _():
        qx_ref[...] = z.astype(BF16)


def fox_inproj(x, g, w, wf_t, bf, tm, tn):
    t, d = x.shape
    c = MIXER_WIDTH
    n_part = c // tn
    assert tn == XATTN_WIDTH

    def part(p):
        return lambda i, j: (i, jnp.clip(j - p * n_part, 0, n_part - 1))

    return pl.pallas_call(
        functools.partial(_fox_inproj_kernel, n_part=n_part),
        grid=(t // tm, 3 * n_part + 1),
        in_specs=[pl.BlockSpec((tm, d), lambda i, j: (i, 0)),
                  pl.BlockSpec((1, d), lambda i, j: (0, 0)),
                  pl.BlockSpec((d, tn), lambda i, j: (0, j)),
                  pl.BlockSpec((LOGF_ROWS, d), lambda i, j: (0, 0)),
                  pl.BlockSpec((LOGF_ROWS, 1), lambda i, j: (0, 0))],
        out_specs=[pl.BlockSpec((tm, tn), part(0)),
                   pl.BlockSpec((tm, tn), part(1)),
                   pl.BlockSpec((tm, tn), part(1)),
                   pl.BlockSpec((tm, tn), part(2)),
                   pl.BlockSpec((tm, tn), part(2)),
                   pl.BlockSpec((tm, XATTN_WIDTH), lambda i, j: (i, 0)),
                   pl.BlockSpec((LOGF_ROWS, tm), lambda i, j: (0, i))],
        out_shape=[jax.ShapeDtypeStruct((t, c), BF16),
                   jax.ShapeDtypeStruct((t, c), F32),
                   jax.ShapeDtypeStruct((t, c), BF16),
                   jax.ShapeDtypeStruct((t, c), F32),
                   jax.ShapeDtypeStruct((t, c), BF16),
                   jax.ShapeDtypeStruct((t, XATTN_WIDTH), BF16),
                   jax.ShapeDtypeStruct((LOGF_ROWS, t), F32)],
        scratch_shapes=[pltpu.VMEM((tm, d), BF16)],
        compiler_params=_params(("arbitrary", "arbitrary")),
        name="fox_inproj",
    )(x, g, w, wf_t, bf)


def _neg_cumsum_kernel(x_ref, o_ref):
    x = x_ref[...]
    n = x.shape[-1]
    lane = lax.broadcasted_iota(jnp.int32, x.shape, x.ndim - 1)
    d = 1
    while d < n:
        x = x + jnp.where(lane >= d, pltpu.roll(x, d, axis=x.ndim - 1), 0.0)
        d *= 2
    o_ref[...] = -x


def neg_cumsum(x, nseq):
    r, total = x.shape
    seg = total // nseq
    return pl.pallas_call(
        _neg_cumsum_kernel,
        grid=(nseq,),
        in_specs=[pl.BlockSpec((r, seg), lambda b: (0, b))],
        out_specs=pl.BlockSpec((r, seg), lambda b: (0, b)),
        out_shape=jax.ShapeDtypeStruct((r, total), F32),
        compiler_params=_params(("arbitrary",)),
        name="neg_cumsum",
    )(x)


def _fox_attn_kernel(q_ref, k_ref, v_ref, nc_ref, o_ref, m_ref, l_ref, acc_ref, *, blk):
    i = pl.program_id(2)
    q = q_ref[0]

    m_ref[...] = jnp.full(m_ref.shape, -jnp.inf, F32)
    l_ref[...] = jnp.zeros(l_ref.shape, F32)
    acc_ref[...] = jnp.zeros(acc_ref.shape, F32)

    def step(j, diagonal):
        start = pl.multiple_of(j * blk, blk)
        s = _dot_nt(q, k_ref[0, pl.ds(start, blk), :]) + nc_ref[0, :, pl.ds(start, blk)]
        if diagonal:
            row = lax.broadcasted_iota(jnp.int32, s.shape, 0)
            col = lax.broadcasted_iota(jnp.int32, s.shape, 1)
            s = jnp.where(col <= row, s, -jnp.inf)
        m_prev = m_ref[...]
        m_new = jnp.maximum(m_prev, jnp.max(s, axis=-1, keepdims=True))
        alpha = jnp.exp(m_prev - m_new)
        p = jnp.exp(s - m_new)
        l_ref[...] = alpha * l_ref[...] + jnp.sum(p, axis=-1, keepdims=True)
        acc_ref[...] = alpha * acc_ref[...] + _dot(p.astype(BF16), v_ref[0, pl.ds(start, blk), :])
        m_ref[...] = m_new

    def body(j, carry):
        step(j, False)
        return carry

    lax.fori_loop(0, i, body, 0)
    step(i, True)
    o_ref[0] = (acc_ref[...] / l_ref[...]).astype(BF16)


def fox_attention(q, k, v, nc, blk):
    b, s, w = q.shape
    h = w // CHUNK_HEAD
    return pl.pallas_call(
        functools.partial(_fox_attn_kernel, blk=blk),
        grid=(b, h, s // blk),
        in_specs=[pl.BlockSpec((1, blk, CHUNK_HEAD), lambda bi, hi, i: (bi, i, hi)),
                  pl.BlockSpec((1, s, CHUNK_HEAD), lambda bi, hi, i: (bi, 0, hi)),
                  pl.BlockSpec((1, s, CHUNK_HEAD), lambda bi, hi, i: (bi, 0, hi)),
                  pl.BlockSpec((1, 1, s), lambda bi, hi, i: (bi * LOGF_ROWS + hi, 0, 0))],
        out_specs=pl.BlockSpec((1, blk, CHUNK_HEAD), lambda bi, hi, i: (bi, i, hi)),
        out_shape=jax.ShapeDtypeStruct((b, s, w), BF16),
        scratch_shapes=[pltpu.VMEM((blk, 1), F32), pltpu.VMEM((blk, 1), F32),
                        pltpu.VMEM((blk, CHUNK_HEAD), F32)],
        compiler_params=_params(("arbitrary", "arbitrary", "arbitrary")),
        name="fox_attention",
    )(q, k, v, nc)


def _fox_decode_kernel(q_ref, kc_ref, vc_ref, kn_ref, vn_ref, nc_ref, o_ref, *, past, t):
    q = q_ref[0]
    s1 = _dot_nt(q, kc_ref[0].astype(BF16)) + nc_ref[0, :, 0:past]
    s2 = _dot_nt(q, kn_ref[0]) + nc_ref[0, :, past:past + t]
    row = lax.broadcasted_iota(jnp.int32, s2.shape, 0)
    col = lax.broadcasted_iota(jnp.int32, s2.shape, 1)
    s2 = jnp.where(col <= row, s2, -jnp.inf)
    m = jnp.maximum(jnp.max(s1, axis=-1, keepdims=True), jnp.max(s2, axis=-1, keepdims=True))
    p1 = jnp.exp(s1 - m)
    p2 = jnp.exp(s2 - m)
    l = jnp.sum(p1, axis=-1, keepdims=True) + jnp.sum(p2, axis=-1, keepdims=True)
    o = _dot(p1.astype(BF16), vc_ref[0].astype(BF16)) + _dot(p2.astype(BF16), vn_ref[0])
    o_ref[0] = (o / l).astype(BF16)


def fox_decode_attention(q, kc, vc, kn, vn, nc):
    b, t, w = q.shape
    past = kc.shape[1]
    h = w // CHUNK_HEAD
    npad = nc.shape[-1]
    new = pl.BlockSpec((1, t, CHUNK_HEAD), lambda bi, hi: (bi, 0, hi))
    old = pl.BlockSpec((1, past, CHUNK_HEAD), lambda bi, hi: (bi, 0, hi))
    return pl.pallas_call(
        functools.partial(_fox_decode_kernel, past=past, t=t),
        grid=(b, h),
        in_specs=[new, old, old, new, new,
                  pl.BlockSpec((1, 1, npad), lambda bi, hi: (bi * LOGF_ROWS + hi, 0, 0))],
        out_specs=new,
        out_shape=jax.ShapeDtypeStruct((b, t, w), BF16),
        compiler_params=_params(("arbitrary", "arbitrary")),
        name="fox_decode_attention",
    )(q, kc, vc, kn, vn, nc)


def _mem_attn_kernel(q_ref, mk_ref, mv_ref, o_ref):
    for h in range(XATTN_HEADS):
        sl = slice(h * CHUNK_HEAD, (h + 1) * CHUNK_HEAD)
        s = _dot_nt(q_ref[0, :, sl], mk_ref[0, :, sl]) * ATTN_SCALE
        p = jnp.exp(s - jnp.max(s, axis=-1, keepdims=True))
        l = jnp.sum(p, axis=-1, keepdims=True)
        o = _dot(p.astype(BF16), mv_ref[0, :, sl])
        o_ref[0, :, sl] = (o / l).astype(BF16)


def mem_attention(qx, mk, mv, tm):
    b, t, w = qx.shape
    nm = mk.shape[1]
    return pl.pallas_call(
        _mem_attn_kernel,
        grid=(b, t // tm),
        in_specs=[pl.BlockSpec((1, tm, w), lambda bi, i: (bi, i, 0)),
                  pl.BlockSpec((1, nm, w), lambda bi, i: (bi, 0, 0)),
                  pl.BlockSpec((1, nm, w), lambda bi, i: (bi, 0, 0))],
        out_specs=pl.BlockSpec((1, tm, w), lambda bi, i: (bi, i, 0)),
        out_shape=jax.ShapeDtypeStruct((b, t, w), BF16),
        compiler_params=_params(("arbitrary", "arbitrary")),
        name="mem_attention",
    )(qx, mk, mv)


def _out_proj_kernel(x_ref, mix_ref, xo_ref, w1_ref, w2_ref, o_ref):
    o_ref[...] = x_ref[...] + _dot(mix_ref[...], w1_ref[...]) + _dot(xo_ref[...], w2_ref[...])


def out_proj(x, mix, xo, w, tm):
    t, d = x.shape
    c = mix.shape[1]
    return pl.pallas_call(
        _out_proj_kernel,
        grid=(t // tm,),
        in_specs=[pl.BlockSpec((tm, d), lambda i: (i, 0)),
                  pl.BlockSpec((tm, c), lambda i: (i, 0)),
                  pl.BlockSpec((tm, XATTN_WIDTH), lambda i: (i, 0)),
                  pl.BlockSpec((c, d), lambda i: (0, 0)),
                  pl.BlockSpec((XATTN_WIDTH, d), lambda i: (c // XATTN_WIDTH, 0))],
        out_specs=pl.BlockSpec((tm, d), lambda i: (i, 0)),
        out_shape=jax.ShapeDtypeStruct((t, d), F32),
        compiler_params=_params(("arbitrary",)),
        name="out_proj",
    )(x, mix, xo, w, w)


def _mlp_kernel(x_ref, g_ref, wu_ref, wd_ref, gf_ref, o_ref, xn_ref, *, final_norm):
    f = pl.program_id(1)

    @pl.when(f == 0)
    def _():
        x = x_ref[...]
        xn_ref[...] = _rms(x, g_ref[...]).astype(BF16)
        o_ref[...] = x

    h = jnp.maximum(_dot(xn_ref[...], wu_ref[...]), 0.0)
    o_ref[...] += _dot((h * h).astype(BF16), wd_ref[...])

    if final_norm:
        @pl.when(f == pl.num_programs(1) - 1)
        def _():
            o_ref[...] = _rms(o_ref[...], gf_ref[...])


def mlp(x, g, w_up, w_down, g_final, tm, tf, final_norm):
    t, d = x.shape
    ff = w_up.shape[1]
    return pl.pallas_call(
        functools.partial(_mlp_kernel, final_norm=final_norm),
        grid=(t // tm, ff // tf),
        in_specs=[pl.BlockSpec((tm, d), lambda i, f: (i, 0)),
                  pl.BlockSpec((1, d), lambda i, f: (0, 0)),
                  pl.BlockSpec((d, tf), lambda i, f: (0, f)),
                  pl.BlockSpec((tf, d), lambda i, f: (f, 0)),
                  pl.BlockSpec((1, d), lambda i, f: (0, 0))],
        out_specs=pl.BlockSpec((tm, d), lambda i, f: (i, 0)),
        out_shape=jax.ShapeDtypeStruct((t, d), F32),
        scratch_shapes=[pltpu.VMEM((tm, d), BF16)],
        compiler_params=_params(("arbitrary", "arbitrary")),
        name="mlp",
    )(x, g, w_up, w_down, g_final)


def _token_tile(t, pref):
    return pref if t % pref == 0 else t


def kernel(x_prompt, x_sample, cache_mem_k, cache_mem_v, state_conv, cache_fox_k, cache_fox_v, cache_fox_logf, mem_prompt, g_mix, g_mem, w_mem_k, w_mem_v, w_in_conv, conv_w, conv_b, conv_ln_g, conv_ln_b, w_in_fox, b_fox_f, w_out, g_mlp, w_up, w_down, g_final):
    bp, sp, d = x_prompt.shape
    bs, ss, _ = x_sample.shape
    depth = g_mix.shape[0]
    n_mem = mem_prompt.shape[1]
    past = cache_fox_k.shape[2]
    c = MIXER_WIDTH

    groups = {
        "p": dict(y=x_prompt.reshape(bp * sp, d), b=bp, s=sp, tm=512, tconv=128),
        "s": dict(y=x_sample.reshape(bs * ss, d), b=bs, s=ss, tm=bs * ss, tconv=ss),
    }
    mem_flat = mem_prompt.reshape(bp * n_mem, d)
    row = lambda a: a.reshape(1, -1)

    mem_k_p, mem_v_p, conv_p, conv_s = [], [], [], []
    fk, fv, fl = {"p": [], "s": []}, {"p": [], "s": []}, {"p": [], "s": []}

    for i in range(depth):
        j = i // 2
        w_kv = jnp.concatenate([w_mem_k[i], w_mem_v[i]], axis=1).astype(BF16)
        kv = rms_matmul(mem_flat, row(g_mem[i]), w_kv, XATTN_WIDTH)
        mk_p = kv[:, :XATTN_WIDTH].reshape(bp, n_mem, XATTN_WIDTH)
        mv_p = kv[:, XATTN_WIDTH:].reshape(bp, n_mem, XATTN_WIDTH)
        mem_k_p.append(mk_p.reshape(bp, n_mem, XATTN_HEADS, CHUNK_HEAD))
        mem_v_p.append(mv_p.reshape(bp, n_mem, XATTN_HEADS, CHUNK_HEAD))
        mem = {"p": (mk_p.astype(BF16), mv_p.astype(BF16)),
               "s": (cache_mem_k[i].reshape(bs, n_mem, XATTN_WIDTH).astype(BF16),
                     cache_mem_v[i].reshape(bs, n_mem, XATTN_WIDTH).astype(BF16))}
        w_o = w_out[i].astype(BF16)
        w_u = w_up[i].astype(BF16)
        w_d = w_down[i].astype(BF16)

        if i % 2 == 0:
            w_in = w_in_conv[j].astype(BF16)
        else:
            wf = w_in_fox[j]
            w_in = jnp.concatenate([wf[:, :3 * c], wf[:, 3 * c + FOX_HEADS:]], axis=1).astype(BF16)
            wf_t = jnp.zeros((LOGF_ROWS, d), BF16).at[:FOX_HEADS].set(wf[:, 3 * c:3 * c + FOX_HEADS].T.astype(BF16))
            bf = jnp.zeros((LOGF_ROWS, 1), F32).at[:FOX_HEADS, 0].set(b_fox_f[j])

        for name, grp in groups.items():
            y, b, s, tm = grp["y"], grp["b"], grp["s"], grp["tm"]
            if i % 2 == 0:
                u, qx = conv_inproj(y, row(g_mix[i]), w_in, tm, XATTN_WIDTH)
                u = u.reshape(b, s, c)
                if name == "p":
                    hist = jnp.zeros((b, CONV_HIST, c), F32)
                    conv_p.append(u[:, s - CONV_STATE:])
                else:
                    st = state_conv[j]
                    hist = jnp.concatenate([jnp.zeros((b, CONV_HIST - CONV_STATE, c), F32), st], axis=1)
                    conv_s.append(jnp.concatenate([st, u], axis=1)[:, s:])
                mix = conv_mixer(u, hist, conv_w[j], row(conv_b[j]), row(conv_ln_g[j]), row(conv_ln_b[j]),
                                 grp["tconv"])
            else:
                q, k_f, k_b, v_f, v_b, qx, lf_t = fox_inproj(y, row(g_mix[i]), w_in, wf_t, bf, tm, XATTN_WIDTH)
                fk[name].append(k_f.reshape(b, s, FOX_HEADS, CHUNK_HEAD))
                fv[name].append(v_f.reshape(b, s, FOX_HEADS, CHUNK_HEAD))
                fl[name].append(lf_t[:FOX_HEADS].reshape(FOX_HEADS, b, s).transpose(1, 2, 0))
                q = q.reshape(b, s, c)
                k_b = k_b.reshape(b, s, c)
                v_b = v_b.reshape(b, s, c)
                if name == "p":
                    nc = neg_cumsum(lf_t, b)
                    nc = nc.reshape(LOGF_ROWS, b, s).transpose(1, 0, 2).reshape(b * LOGF_ROWS, 1, s)
                    mix = fox_attention(q, k_b, v_b, nc, 512)
                else:
                    total = past + s
                    padded = -(-total // LANE) * LANE
                    lf_old = jnp.zeros((b, LOGF_ROWS, past), F32).at[:, :FOX_HEADS].set(
                        cache_fox_logf[j].astype(F32).transpose(0, 2, 1))
                    lf_all = jnp.concatenate(
                        [lf_old, lf_t.reshape(LOGF_ROWS, b, s).transpose(1, 0, 2),
                         jnp.zeros((b, LOGF_ROWS, padded - total), F32)], axis=2)
                    lf_all = lf_all.transpose(1, 0, 2).reshape(LOGF_ROWS, b * padded)
                    nc = neg_cumsum(lf_all, b)
                    nc = nc.reshape(LOGF_ROWS, b, padded).transpose(1, 0, 2).reshape(b * LOGF_ROWS, 1, padded)
                    mix = fox_decode_attention(q, cache_fox_k[j].reshape(b, past, c),
                                               cache_fox_v[j].reshape(b, past, c), k_b, v_b, nc)
            qx = qx.reshape(b, s, XATTN_WIDTH)
            xo = mem_attention(qx, mem[name][0], mem[name][1], _token_tile(s, 512))
            y = out_proj(y, mix.reshape(b * s, c), xo.reshape(b * s, XATTN_WIDTH), w_o, _token_tile(b * s, 512))
            y = mlp(y, row(g_mlp[i]), w_u, w_d, row(g_final), _token_tile(b * s, 512), 1024,
                    final_norm=(i == depth - 1))
            grp["y"] = y

    y_prompt = groups["p"]["y"].reshape(bp, sp, d)
    y_sample = groups["s"]["y"].reshape(bs, ss, d)
    return (y_prompt, y_sample, jnp.stack(mem_k_p), jnp.stack(mem_v_p), jnp.stack(conv_p), jnp.stack(conv_s),
            jnp.stack(fk["p"]), jnp.stack(fv["p"]), jnp.stack(fl["p"]),
            jnp.stack(fk["s"]), jnp.stack(fv["s"]), jnp.stack(fl["s"]))
```

```python
import functools

import jax
import jax.numpy as jnp
from jax import lax
from jax.experimental import pallas as pl
from jax.experimental.pallas import tpu as pltpu

F32 = jnp.float32
BF16 = jnp.bfloat16

D_MODEL = 2048
CHUNK_HEAD = 128
XATTN_HEADS = 4
XATTN_WIDTH = XATTN_HEADS * CHUNK_HEAD
MIXER_WIDTH = D_MODEL - XATTN_WIDTH
FOX_HEADS = MIXER_WIDTH // CHUNK_HEAD
CONV_WIDTH = 31
CONV_STATE = CONV_WIDTH - 1
CONV_HIST = 32
D_FF = 4 * D_MODEL
EPS = 1e-6
ATTN_SCALE = CHUNK_HEAD ** -0.5
LOGF_ROWS = 16
LANE = 128
SUBLANE = 8
BIAS_PARTS = 3

VMEM_LIMIT = 56 * 1024 * 1024


def _params(sem):
    return pltpu.CompilerParams(dimension_semantics=sem, vmem_limit_bytes=VMEM_LIMIT)


def _rms(x, g):
    ms = jnp.mean(x * x, axis=-1, keepdims=True)
    return x * lax.rsqrt(ms + EPS) * g


def _sigmoid(x):
    return 1.0 / (1.0 + jnp.exp(-x))


def _dot(a, b):
    return jnp.dot(a, b, preferred_element_type=F32)


def _dot_nt(a, b):
    return lax.dot_general(a, b, (((1,), (1,)), ((), ())), preferred_element_type=F32)


def _rms_matmul_kernel(x_ref, g_ref, w_ref, o_ref):
    xn = _rms(x_ref[...], g_ref[...]).astype(BF16)
    o_ref[...] = _dot(xn, w_ref[...])


def rms_matmul(x, g, w, tn):
    t, d = x.shape
    n = w.shape[1]
    return pl.pallas_call(
        _rms_matmul_kernel,
        grid=(n // tn,),
        in_specs=[pl.BlockSpec((t, d), lambda j: (0, 0)),
                  pl.BlockSpec((1, d), lambda j: (0, 0)),
                  pl.BlockSpec((d, tn), lambda j: (0, j))],
        out_specs=pl.BlockSpec((t, tn), lambda j: (0, j)),
        out_shape=jax.ShapeDtypeStruct((t, n), F32),
        compiler_params=_params(("arbitrary",)),
        name="rms_matmul",
    )(x, g, w)


def _conv_inproj_kernel(x_ref, g_ref, wa_ref, wg_ref, wq_ref, u_ref, qx_ref, xn_ref, *, n_glu):
    j = pl.program_id(1)

    @pl.when(j == 0)
    def _():
        xn_ref[...] = _rms(x_ref[...], g_ref[...]).astype(BF16)

    @pl.when(j < n_glu)
    def _():
        xn = xn_ref[...]
        a = _dot(xn, wa_ref[...])
        gate = _dot(xn, wg_ref[...])
        u_ref[...] = a * _sigmoid(gate)

    @pl.when(j == n_glu)
    def _():
        qx_ref[...] = _dot(xn_ref[...], wq_ref[...]).astype(BF16)


def conv_inproj(x, g, w, tm, tn):
    t, d = x.shape
    c = MIXER_WIDTH
    n_glu = c // tn
    last = n_glu - 1
    return pl.pallas_call(
        functools.partial(_conv_inproj_kernel, n_glu=n_glu),
        grid=(t // tm, n_glu + 1),
        in_specs=[pl.BlockSpec((tm, d), lambda i, j: (i, 0)),
                  pl.BlockSpec((1, d), lambda i, j: (0, 0)),
                  pl.BlockSpec((d, tn), lambda i, j: (0, jnp.minimum(j, last))),
                  pl.BlockSpec((d, tn), lambda i, j: (0, n_glu + jnp.minimum(j, last))),
                  pl.BlockSpec((d, XATTN_WIDTH), lambda i, j: (0, 2 * c // XATTN_WIDTH))],
        out_specs=[pl.BlockSpec((tm, tn), lambda i, j: (i, jnp.minimum(j, last))),
                   pl.BlockSpec((tm, XATTN_WIDTH), lambda i, j: (i, 0))],
        out_shape=[jax.ShapeDtypeStruct((t, c), F32),
                   jax.ShapeDtypeStruct((t, XATTN_WIDTH), BF16)],
        scratch_shapes=[pltpu.VMEM((tm, d), BF16)],
        compiler_params=_params(("arbitrary", "arbitrary")),
        name="conv_inproj",
    )(x, g, w, w, w)


def _conv_mixer_kernel(u_ref, hist_ref, w_ref, b_ref, lg_ref, lb_ref, o_ref, ubuf, cbuf, *, tt, tr, carry):
    i = pl.program_id(1)
    c = u_ref.shape[-1]

    @pl.when(i == 0)
    def _():
        ubuf[0:CONV_HIST, :] = hist_ref[0]

    ubuf[CONV_HIST:CONV_HIST + tt, :] = u_ref[0]

    off = CONV_HIST - CONV_STATE
    rows = tr + CONV_HIST
    for c0 in range(0, c, LANE):
        for r0 in range(0, tt, tr):
            window = ubuf[r0:r0 + rows, c0:c0 + LANE]
            acc = jnp.zeros((tr, LANE), F32) + b_ref[:, c0:c0 + LANE]
            for res in range(SUBLANE):
                shifted = pltpu.roll(window, rows - res, axis=0) if res else window
                for k in range(CONV_WIDTH):
                    if (off + k) % SUBLANE == res:
                        a0 = off + k - res
                        acc = acc + w_ref[k:k + 1, c0:c0 + LANE] * shifted[a0:a0 + tr]
            cbuf[r0:r0 + tr, c0:c0 + LANE] = acc

    y = cbuf[...]
    yc = y - jnp.mean(y, axis=-1, keepdims=True)
    yn = yc * lax.rsqrt(jnp.mean(yc * yc, axis=-1, keepdims=True) + EPS)
    yn = yn * lg_ref[...] + lb_ref[...]
    o_ref[0] = (yn * _sigmoid(yn)).astype(BF16)

    if carry:
        ubuf[0:CONV_HIST, :] = ubuf[tt:tt + CONV_HIST, :]


def conv_mixer(u, hist, conv_w, conv_b, ln_g, ln_b, tt):
    b, t, c = u.shape
    nblk = t // tt
    tr = min(tt, 64)
    return pl.pallas_call(
        functools.partial(_conv_mixer_kernel, tt=tt, tr=tr, carry=nblk > 1),
        grid=(b, nblk),
        in_specs=[pl.BlockSpec((1, tt, c), lambda bi, i: (bi, i, 0)),
                  pl.BlockSpec((1, CONV_HIST, c), lambda bi, i: (bi, 0, 0)),
                  pl.BlockSpec((CONV_WIDTH, c), lambda bi, i: (0, 0)),
                  pl.BlockSpec((1, c), lambda bi, i: (0, 0)),
                  pl.BlockSpec((1, c), lambda bi, i: (0, 0)),
                  pl.BlockSpec((1, c), lambda bi, i: (0, 0))],
        out_specs=pl.BlockSpec((1, tt, c), lambda bi, i: (bi, i, 0)),
        out_shape=jax.ShapeDtypeStruct((b, t, c), BF16),
        scratch_shapes=[pltpu.VMEM((CONV_HIST + tt, c), F32), pltpu.VMEM((tt, c), F32)],
        compiler_params=_params(("arbitrary", "arbitrary")),
        name="conv_mixer",
    )(u, hist, conv_w, conv_b, ln_g, ln_b)


def _fox_inproj_kernel(x_ref, g_ref, w_ref, wf_ref, bf_ref,
                       q_ref, kf_ref, kb_ref, vf_ref, vb_ref, qx_ref, lf_ref, xn_ref, *, n_part, transposed):
    j = pl.program_id(1)

    def oriented(a):
        return a.T if transposed else a

    @pl.when(j == 0)
    def _():
        xn = _rms(x_ref[...], g_ref[...]).astype(BF16)
        xn_ref[...] = xn
        f = _dot_nt(wf_ref[...], xn) + bf_ref[...]
        lf_ref[...] = jnp.minimum(f, 0.0) - jnp.log1p(jnp.exp(-jnp.abs(f)))

    z = _dot(xn_ref[...], w_ref[...])

    @pl.when(j < n_part)
    def _():
        q_ref[...] = oriented(z * ATTN_SCALE).astype(BF16)

    heads_per_tile = z.shape[1] // CHUNK_HEAD

    def store_heads(ref, part):
        for p in range(n_part):
            @pl.when(j == part * n_part + p)
            def _():
                for hh in range(heads_per_tile):
                    ref[:, p * heads_per_tile + hh, :] = z[:, hh * CHUNK_HEAD:(hh + 1) * CHUNK_HEAD]

    store_heads(kf_ref, 1)
    store_heads(vf_ref, 2)

    @pl.when(jnp.logical_and(j >= n_part, j < 2 * n_part))
    def _():
        kb_ref[...] = z.astype(BF16)

    @pl.when(jnp.logical_and(j >= 2 * n_part, j < 3 * n_part))
    def _():
        vb_ref[...] = oriented(z).astype(BF16)

    @pl.when(j == 3 * n_part)
    def _():
        qx_ref[...] = z.astype(BF16)


def fox_inproj(x, g, w, wf_t, bf, tm, tn, transposed):
    t, d = x.shape
    c = MIXER_WIDTH
    n_part = c // tn
    assert tn == XATTN_WIDTH

    def part(p):
        return lambda i, j: (i, jnp.clip(j - p * n_part, 0, n_part - 1))

    def part_t(p):
        return lambda i, j: (jnp.clip(j - p * n_part, 0, n_part - 1), i)

    if transposed:
        qv_shape = jax.ShapeDtypeStruct((c, t), BF16)
        q_spec, v_spec = pl.BlockSpec((tn, tm), part_t(0)), pl.BlockSpec((tn, tm), part_t(2))
    else:
        qv_shape = jax.ShapeDtypeStruct((t, c), BF16)
        q_spec, v_spec = pl.BlockSpec((tm, tn), part(0)), pl.BlockSpec((tm, tn), part(2))

    return pl.pallas_call(
        functools.partial(_fox_inproj_kernel, n_part=n_part, transposed=transposed),
        grid=(t // tm, 3 * n_part + 1),
        in_specs=[pl.BlockSpec((tm, d), lambda i, j: (i, 0)),
                  pl.BlockSpec((1, d), lambda i, j: (0, 0)),
                  pl.BlockSpec((d, tn), lambda i, j: (0, j)),
                  pl.BlockSpec((LOGF_ROWS, d), lambda i, j: (0, 0)),
                  pl.BlockSpec((LOGF_ROWS, 1), lambda i, j: (0, 0))],
        out_specs=[q_spec,
                   pl.BlockSpec((tm, FOX_HEADS, CHUNK_HEAD), lambda i, j: (i, 0, 0)),
                   pl.BlockSpec((tm, tn), part(1)),
                   pl.BlockSpec((tm, FOX_HEADS, CHUNK_HEAD), lambda i, j: (i, 0, 0)),
                   v_spec,
                   pl.BlockSpec((tm, XATTN_WIDTH), lambda i, j: (i, 0)),
                   pl.BlockSpec((LOGF_ROWS, tm), lambda i, j: (0, i))],
        out_shape=[qv_shape,
                   jax.ShapeDtypeStruct((t, FOX_HEADS, CHUNK_HEAD), F32),
                   jax.ShapeDtypeStruct((t, c), BF16),
                   jax.ShapeDtypeStruct((t, FOX_HEADS, CHUNK_HEAD), F32),
                   qv_shape,
                   jax.ShapeDtypeStruct((t, XATTN_WIDTH), BF16),
                   jax.ShapeDtypeStruct((LOGF_ROWS, t), F32)],
        scratch_shapes=[pltpu.VMEM((tm, d), BF16)],
        compiler_params=_params(("arbitrary", "arbitrary")),
        name="fox_inproj",
    )(x, g, w, wf_t, bf)


def _neg_cumsum_kernel(x_ref, o_ref):
    x = x_ref[...]
    n = x.shape[-1]
    lane = lax.broadcasted_iota(jnp.int32, x.shape, x.ndim - 1)
    d = 1
    while d < n:
        x = x + jnp.where(lane >= d, pltpu.roll(x, d, axis=x.ndim - 1), 0.0)
        d *= 2
    o_ref[...] = -x


def neg_cumsum(x, nseq):
    r, total = x.shape
    seg = total // nseq
    return pl.pallas_call(
        _neg_cumsum_kernel,
        grid=(nseq,),
        in_specs=[pl.BlockSpec((r, seg), lambda b: (0, b))],
        out_specs=pl.BlockSpec((r, seg), lambda b: (0, b)),
        out_shape=jax.ShapeDtypeStruct((r, total), F32),
        compiler_params=_params(("arbitrary",)),
        name="neg_cumsum",
    )(x)


def _key_bias_kernel(nc_ref, o_ref):
    x = nc_ref[...]
    ts = x.shape[1]
    xt = jnp.concatenate([x, jnp.zeros((LANE - LOGF_ROWS, ts), F32)], axis=0).T
    lane = lax.broadcasted_iota(jnp.int32, (ts, LANE), 1)
    for h in range(FOX_HEADS):
        col = xt[:, h:h + 1]
        hi = col.astype(BF16).astype(F32)
        rest = col - hi
        mid = rest.astype(BF16).astype(F32)
        lo = rest - mid
        tile = jnp.where(lane == 0, hi, jnp.where(lane == 1, mid, jnp.where(lane == 2, lo, 0.0)))
        o_ref[0, :, h * CHUNK_HEAD:(h + 1) * CHUNK_HEAD] = tile.astype(BF16)


def key_bias(nc, b, s, ts):
    return pl.pallas_call(
        _key_bias_kernel,
        grid=(b, s // ts),
        in_specs=[pl.BlockSpec((LOGF_ROWS, ts), lambda bi, i: (0, bi * (s // ts) + i))],
        out_specs=pl.BlockSpec((1, ts, MIXER_WIDTH), lambda bi, i: (bi, i, 0)),
        out_shape=jax.ShapeDtypeStruct((b, s, MIXER_WIDTH), BF16),
        compiler_params=_params(("arbitrary", "arbitrary")),
        name="key_bias",
    )(nc)


def _fox_attn_kernel(qt_ref, k_ref, kb_ref, vt_ref, o_ref, qa_ref, m_ref, l_ref, acc_ref, *, blk, hp):
    i = pl.program_id(2)

    ones = (lax.broadcasted_iota(jnp.int32, (CHUNK_HEAD, blk), 0) < BIAS_PARTS).astype(BF16)
    for hh in range(hp):
        qa_ref[hh, 0:CHUNK_HEAD, :] = qt_ref[hh * CHUNK_HEAD:(hh + 1) * CHUNK_HEAD, :]
        qa_ref[hh, CHUNK_HEAD:2 * CHUNK_HEAD, :] = ones
    m_ref[...] = jnp.full(m_ref.shape, -jnp.inf, F32)
    l_ref[...] = jnp.zeros(l_ref.shape, F32)
    acc_ref[...] = jnp.zeros(acc_ref.shape, F32)

    def step(j, diagonal):
        start = pl.multiple_of(j * blk, blk)
        heads = [slice(hh * CHUNK_HEAD, (hh + 1) * CHUNK_HEAD) for hh in range(hp)]
        scores = []
        for hh, sl in enumerate(heads):
            ka = jnp.concatenate([k_ref[0, pl.ds(start, blk), sl], kb_ref[0, pl.ds(start, blk), sl]], axis=1)
            scores.append(_dot(ka, qa_ref[hh]))
        for hh, sl in enumerate(heads):
            s = scores[hh]
            if diagonal:
                key = lax.broadcasted_iota(jnp.int32, s.shape, 0)
                qry = lax.broadcasted_iota(jnp.int32, s.shape, 1)
                s = jnp.where(key <= qry, s, -jnp.inf)
            m_prev = m_ref[hh]
            m_new = jnp.maximum(m_prev, jnp.max(s, axis=0, keepdims=True))
            alpha = jnp.exp(m_prev - m_new)
            p = jnp.exp(s - m_new)
            l_ref[hh] = alpha * l_ref[hh] + jnp.sum(p, axis=0, keepdims=True)
            acc_ref[hh] = alpha * acc_ref[hh] + _dot(vt_ref[sl, pl.ds(start, blk)], p.astype(BF16))
            m_ref[hh] = m_new

    def body(j, carry):
        step(j, False)
        return carry

    lax.fori_loop(0, i, body, 0)
    step(i, True)
    for hh in range(hp):
        out_t = acc_ref[hh] * (1.0 / l_ref[hh])
        o_ref[0, :, hh * CHUNK_HEAD:(hh + 1) * CHUNK_HEAD] = out_t.T.astype(BF16)


def fox_attention(qt, k, kb, vt, blk, hp):
    b, s, w = k.shape
    h = w // CHUNK_HEAD
    wide = hp * CHUNK_HEAD
    nq = s // blk
    return pl.pallas_call(
        functools.partial(_fox_attn_kernel, blk=blk, hp=hp),
        grid=(b, h // hp, nq),
        in_specs=[pl.BlockSpec((wide, blk), lambda bi, hi, i: (hi, bi * nq + i)),
                  pl.BlockSpec((1, s, wide), lambda bi, hi, i: (bi, 0, hi)),
                  pl.BlockSpec((1, s, wide), lambda bi, hi, i: (bi, 0, hi)),
                  pl.BlockSpec((wide, s), lambda bi, hi, i: (hi, bi))],
        out_specs=pl.BlockSpec((1, blk, wide), lambda bi, hi, i: (bi, i, hi)),
        out_shape=jax.ShapeDtypeStruct((b, s, w), BF16),
        scratch_shapes=[pltpu.VMEM((hp, 2 * CHUNK_HEAD, blk), BF16),
                        pltpu.VMEM((hp, 1, blk), F32), pltpu.VMEM((hp, 1, blk), F32),
                        pltpu.VMEM((hp, CHUNK_HEAD, blk), F32)],
        compiler_params=_params(("arbitrary", "arbitrary", "arbitrary")),
        name="fox_attention",
    )(qt, k, kb, vt)


def _fox_decode_kernel(q_ref, kc_ref, vc_ref, kn_ref, vn_ref, nc_ref, o_ref, *, past, t, heads):
    row = lax.broadcasted_iota(jnp.int32, (t, t), 0)
    col = lax.broadcasted_iota(jnp.int32, (t, t), 1)
    for h in range(heads):
        sl = slice(h * CHUNK_HEAD, (h + 1) * CHUNK_HEAD)
        q = q_ref[0, :, sl]
        s1 = _dot_nt(q, kc_ref[0, :, h, :].astype(BF16)) + nc_ref[h, :, 0:past]
        s2 = _dot_nt(q, kn_ref[0, :, sl]) + nc_ref[h, :, past:past + t]
        s2 = jnp.where(col <= row, s2, -jnp.inf)
        m = jnp.maximum(jnp.max(s1, axis=-1, keepdims=True), jnp.max(s2, axis=-1, keepdims=True))
        p1 = jnp.exp(s1 - m)
        p2 = jnp.exp(s2 - m)
        l = jnp.sum(p1, axis=-1, keepdims=True) + jnp.sum(p2, axis=-1, keepdims=True)
        o = _dot(p1.astype(BF16), vc_ref[0, :, h, :].astype(BF16)) + _dot(p2.astype(BF16), vn_ref[0, :, sl])
        o_ref[0, :, sl] = (o / l).astype(BF16)


def fox_decode_attention(q, kc, vc, kn, vn, nc):
    b, t, w = q.shape
    _, past, h, dh = kc.shape
    npad = nc.shape[-1]
    new = pl.BlockSpec((1, t, w), lambda bi: (bi, 0, 0))
    old = pl.BlockSpec((1, past, h, dh), lambda bi: (bi, 0, 0, 0))
    return pl.pallas_call(
        functools.partial(_fox_decode_kernel, past=past, t=t, heads=h),
        grid=(b,),
        in_specs=[new, old, old, new, new,
                  pl.BlockSpec((LOGF_ROWS, 1, npad), lambda bi: (bi, 0, 0))],
        out_specs=new,
        out_shape=jax.ShapeDtypeStruct((b, t, w), BF16),
        compiler_params=_params(("arbitrary",)),
        name="fox_decode_attention",
    )(q, kc, vc, kn, vn, nc)


def _mem_attn_kernel(q_ref, mk_ref, mv_ref, o_ref):
    for h in range(XATTN_HEADS):
        sl = slice(h * CHUNK_HEAD, (h + 1) * CHUNK_HEAD)
        s = _dot_nt(q_ref[0, :, sl], mk_ref[0, :, sl]) * ATTN_SCALE
        p = jnp.exp(s - jnp.max(s, axis=-1, keepdims=True))
        l = jnp.sum(p, axis=-1, keepdims=True)
        o = _dot(p.astype(BF16), mv_ref[0, :, sl])
        o_ref[0, :, sl] = (o / l).astype(BF16)


def mem_attention(qx, mk, mv, tm):
    b, t, w = qx.shape
    nm = mk.shape[1]
    return pl.pallas_call(
        _mem_attn_kernel,
        grid=(b, t // tm),
        in_specs=[pl.BlockSpec((1, tm, w), lambda bi, i: (bi, i, 0)),
                  pl.BlockSpec((1, nm, w), lambda bi, i: (bi, 0, 0)),
                  pl.BlockSpec((1, nm, w), lambda bi, i: (bi, 0, 0))],
        out_specs=pl.BlockSpec((1, tm, w), lambda bi, i: (bi, i, 0)),
        out_shape=jax.ShapeDtypeStruct((b, t, w), BF16),
        compiler_params=_params(("arbitrary", "arbitrary")),
        name="mem_attention",
    )(qx, mk, mv)


def _out_proj_kernel(x_ref, mix_ref, xo_ref, w1_ref, w2_ref, o_ref):
    o_ref[...] = x_ref[...] + _dot(mix_ref[...], w1_ref[...]) + _dot(xo_ref[...], w2_ref[...])


def out_proj(x, mix, xo, w, layer, tm):
    t, d = x.shape
    c = mix.shape[1]
    return pl.pallas_call(
        _out_proj_kernel,
        grid=(t // tm,),
        in_specs=[pl.BlockSpec((tm, d), lambda i: (i, 0)),
                  pl.BlockSpec((tm, c), lambda i: (i, 0)),
                  pl.BlockSpec((tm, XATTN_WIDTH), lambda i: (i, 0)),
                  pl.BlockSpec((None, c, d), lambda i: (layer, 0, 0)),
                  pl.BlockSpec((None, XATTN_WIDTH, d), lambda i: (layer, c // XATTN_WIDTH, 0))],
        out_specs=pl.BlockSpec((tm, d), lambda i: (i, 0)),
        out_shape=jax.ShapeDtypeStruct((t, d), F32),
        compiler_params=_params(("arbitrary",)),
        name="out_proj",
    )(x, mix, xo, w, w)


def _mlp_kernel(x_ref, g_ref, wu_ref, wd_ref, gf_ref, o_ref, xn_ref, *, final_norm):
    f = pl.program_id(1)

    @pl.when(f == 0)
    def _():
        x = x_ref[...]
        xn_ref[...] = _rms(x, g_ref[...]).astype(BF16)
        o_ref[...] = x

    h = jnp.maximum(_dot(xn_ref[...], wu_ref[...]), 0.0)
    o_ref[...] += _dot((h * h).astype(BF16), wd_ref[...])

    if final_norm:
        @pl.when(f == pl.num_programs(1) - 1)
        def _():
            o_ref[...] = _rms(o_ref[...], gf_ref[...])


def mlp(x, g, w_up, w_down, g_final, layer, tm, tf, final_norm):
    t, d = x.shape
    ff = w_up.shape[2]
    return pl.pallas_call(
        functools.partial(_mlp_kernel, final_norm=final_norm),
        grid=(t // tm, ff // tf),
        in_specs=[pl.BlockSpec((tm, d), lambda i, f: (i, 0)),
                  pl.BlockSpec((1, d), lambda i, f: (0, 0)),
                  pl.BlockSpec((None, d, tf), lambda i, f: (layer, 0, f)),
                  pl.BlockSpec((None, tf, d), lambda i, f: (layer, f, 0)),
                  pl.BlockSpec((1, d), lambda i, f: (0, 0))],
        out_specs=pl.BlockSpec((tm, d), lambda i, f: (i, 0)),
        out_shape=jax.ShapeDtypeStruct((t, d), F32),
        scratch_shapes=[pltpu.VMEM((tm, d), BF16)],
        compiler_params=_params(("arbitrary", "arbitrary")),
        name="mlp",
    )(x, g, w_up, w_down, g_final)


def _token_tile(t, pref):
    return pref if t % pref == 0 else t


def kernel(x_prompt, x_sample, cache_mem_k, cache_mem_v, state_conv, cache_fox_k, cache_fox_v, cache_fox_logf, mem_prompt, g_mix, g_mem, w_mem_k, w_mem_v, w_in_conv, conv_w, conv_b, conv_ln_g, conv_ln_b, w_in_fox, b_fox_f, w_out, g_mlp, w_up, w_down, g_final):
    bp, sp, d = x_prompt.shape
    bs, ss, _ = x_sample.shape
    depth = g_mix.shape[0]
    n_mem = mem_prompt.shape[1]
    past = cache_fox_k.shape[2]
    c = MIXER_WIDTH

    groups = {
        "p": dict(y=x_prompt.reshape(bp * sp, d), b=bp, s=sp, tm_conv=1024, tm_fox=512, tconv=128),
        "s": dict(y=x_sample.reshape(bs * ss, d), b=bs, s=ss, tm_conv=bs * ss, tm_fox=bs * ss, tconv=ss),
    }
    mem_flat = mem_prompt.reshape(bp * n_mem, d)
    row = lambda a: a.reshape(1, -1)
    w_o, w_u, w_d = w_out.astype(BF16), w_up.astype(BF16), w_down.astype(BF16)

    mem_k_p, mem_v_p, conv_p, conv_s = [], [], [], []
    fk, fv, fl = {"p": [], "s": []}, {"p": [], "s": []}, {"p": [], "s": []}

    for i in range(depth):
        j = i // 2
        w_kv = jnp.concatenate([w_mem_k[i], w_mem_v[i]], axis=1).astype(BF16)
        kv = rms_matmul(mem_flat, row(g_mem[i]), w_kv, XATTN_WIDTH)
        mk_p = kv[:, :XATTN_WIDTH].reshape(bp, n_mem, XATTN_WIDTH)
        mv_p = kv[:, XATTN_WIDTH:].reshape(bp, n_mem, XATTN_WIDTH)
        mem_k_p.append(mk_p.reshape(bp, n_mem, XATTN_HEADS, CHUNK_HEAD))
        mem_v_p.append(mv_p.reshape(bp, n_mem, XATTN_HEADS, CHUNK_HEAD))
        mem = {"p": (mk_p.astype(BF16), mv_p.astype(BF16)),
               "s": (cache_mem_k[i].reshape(bs, n_mem, XATTN_WIDTH).astype(BF16),
                     cache_mem_v[i].reshape(bs, n_mem, XATTN_WIDTH).astype(BF16))}
        if i % 2 == 0:
            w_in = w_in_conv[j].astype(BF16)
        else:
            wf = w_in_fox[j]
            w_in = jnp.concatenate([wf[:, :3 * c], wf[:, 3 * c + FOX_HEADS:]], axis=1).astype(BF16)
            wf_t = jnp.zeros((LOGF_ROWS, d), BF16).at[:FOX_HEADS].set(wf[:, 3 * c:3 * c + FOX_HEADS].T.astype(BF16))
            bf = jnp.zeros((LOGF_ROWS, 1), F32).at[:FOX_HEADS, 0].set(b_fox_f[j])

        for name, grp in groups.items():
            y, b, s = grp["y"], grp["b"], grp["s"]
            if i % 2 == 0:
                u, qx = conv_inproj(y, row(g_mix[i]), w_in, grp["tm_conv"], XATTN_WIDTH)
                u = u.reshape(b, s, c)
                if name == "p":
                    hist = jnp.zeros((b, CONV_HIST, c), F32)
                    conv_p.append(u[:, s - CONV_STATE:])
                else:
                    st = state_conv[j]
                    hist = jnp.concatenate([jnp.zeros((b, CONV_HIST - CONV_STATE, c), F32), st], axis=1)
                    conv_s.append(jnp.concatenate([st, u], axis=1)[:, s:])
                mix = conv_mixer(u, hist, conv_w[j], row(conv_b[j]), row(conv_ln_g[j]), row(conv_ln_b[j]),
                                 grp["tconv"])
            else:
                q, k_f, k_b, v_f, v_b, qx, lf_t = fox_inproj(y, row(g_mix[i]), w_in, wf_t, bf, grp["tm_fox"],
                                                             XATTN_WIDTH, transposed=(name == "p"))
                fk[name].append(k_f.reshape(b, s, FOX_HEADS, CHUNK_HEAD))
                fv[name].append(v_f.reshape(b, s, FOX_HEADS, CHUNK_HEAD))
                fl[name].append(lf_t[:FOX_HEADS].reshape(FOX_HEADS, b, s).transpose(1, 2, 0))
                k_b = k_b.reshape(b, s, c)
                if name == "p":
                    nc = neg_cumsum(lf_t, b)
                    mix = fox_attention(q, k_b, key_bias(nc, b, s, 512), v_b, 512, 2)
                else:
                    q = q.reshape(b, s, c)
                    v_b = v_b.reshape(b, s, c)
                    total = past + s
                    padded = -(-total // LANE) * LANE
                    lf_old = jnp.zeros((b, LOGF_ROWS, past), F32).at[:, :FOX_HEADS].set(
                        cache_fox_logf[j].astype(F32).transpose(0, 2, 1))
                    lf_all = jnp.concatenate(
                        [lf_old, lf_t.reshape(LOGF_ROWS, b, s).transpose(1, 0, 2),
                         jnp.zeros((b, LOGF_ROWS, padded - total), F32)], axis=2)
                    lf_all = lf_all.transpose(1, 0, 2).reshape(LOGF_ROWS, b * padded)
                    nc = neg_cumsum(lf_all, b)
                    nc = nc.reshape(LOGF_ROWS, b, padded).transpose(1, 0, 2).reshape(b * LOGF_ROWS, 1, padded)
                    mix = fox_decode_attention(q, cache_fox_k[j], cache_fox_v[j], k_b, v_b, nc)
            qx = qx.reshape(b, s, XATTN_WIDTH)
            xo = mem_attention(qx, mem[name][0], mem[name][1], _token_tile(s, 512))
            y = out_proj(y, mix.reshape(b * s, c), xo.reshape(b * s, XATTN_WIDTH), w_o, i, _token_tile(b * s, 512))
            y = mlp(y, row(g_mlp[i]), w_u, w_d, row(g_final), i, _token_tile(b * s, 512), 1024,
                    final_norm=(i == depth - 1))
            grp["y"] = y

    y_prompt = groups["p"]["y"].reshape(bp, sp, d)
    y_sample = groups["s"]["y"].reshape(bs, ss, d)
    return (y_prompt, y_sample, jnp.stack(mem_k_p), jnp.stack(mem_v_p), jnp.stack(conv_p), jnp.stack(conv_s),
            jnp.stack(fk["p"]), jnp.stack(fv["p"]), jnp.stack(fl["p"]),
            jnp.stack(fk["s"]), jnp.stack(fv["s"]), jnp.stack(fl["s"]))
```

```python
import functools

import jax
import jax.numpy as jnp
from jax import lax
from jax.experimental import pallas as pl
from jax.experimental.pallas import tpu as pltpu

F32 = jnp.float32
BF16 = jnp.bfloat16

D_MODEL = 2048
CHUNK_HEAD = 128
XATTN_HEADS = 4
XATTN_WIDTH = XATTN_HEADS * CHUNK_HEAD
MIXER_WIDTH = D_MODEL - XATTN_WIDTH
FOX_HEADS = MIXER_WIDTH // CHUNK_HEAD
CONV_WIDTH = 31
CONV_STATE = CONV_WIDTH - 1
CONV_HIST = 32
D_FF = 4 * D_MODEL
EPS = 1e-6
ATTN_SCALE = CHUNK_HEAD ** -0.5
LOG2E = 1.4426950408889634
LOGF_ROWS = 16
LANE = 128
SUBLANE = 8
BIAS_PARTS = 3

VMEM_LIMIT = 56 * 1024 * 1024


def _params(sem):
    return pltpu.CompilerParams(dimension_semantics=sem, vmem_limit_bytes=VMEM_LIMIT)


def _rms(x, g):
    ms = jnp.mean(x * x, axis=-1, keepdims=True)
    return x * lax.rsqrt(ms + EPS) * g


def _sigmoid(x):
    return 1.0 / (1.0 + jnp.exp(-x))


def _dot(a, b):
    return jnp.dot(a, b, preferred_element_type=F32)


def _dot_nt(a, b):
    return lax.dot_general(a, b, (((1,), (1,)), ((), ())), preferred_element_type=F32)


def _rms_matmul_kernel(x_ref, g_ref, w_ref, o_ref):
    xn = _rms(x_ref[...], g_ref[...]).astype(BF16)
    o_ref[...] = _dot(xn, w_ref[...])


def rms_matmul(x, g, w, tn):
    t, d = x.shape
    n = w.shape[1]
    return pl.pallas_call(
        _rms_matmul_kernel,
        grid=(n // tn,),
        in_specs=[pl.BlockSpec((t, d), lambda j: (0, 0)),
                  pl.BlockSpec((1, d), lambda j: (0, 0)),
                  pl.BlockSpec((d, tn), lambda j: (0, j))],
        out_specs=pl.BlockSpec((t, tn), lambda j: (0, j)),
        out_shape=jax.ShapeDtypeStruct((t, n), F32),
        compiler_params=_params(("arbitrary",)),
        name="rms_matmul",
    )(x, g, w)


def _conv_inproj_kernel(x_ref, g_ref, wa_ref, wg_ref, wq_ref, u_ref, qx_ref, xn_ref, *, n_glu):
    j = pl.program_id(1)

    @pl.when(j == 0)
    def _():
        xn_ref[...] = _rms(x_ref[...], g_ref[...]).astype(BF16)

    @pl.when(j < n_glu)
    def _():
        xn = xn_ref[...]
        a = _dot(xn, wa_ref[...])
        gate = _dot(xn, wg_ref[...])
        u_ref[...] = a * _sigmoid(gate)

    @pl.when(j == n_glu)
    def _():
        qx_ref[...] = _dot(xn_ref[...], wq_ref[...]).astype(BF16)


def conv_inproj(x, g, w, tm, tn):
    t, d = x.shape
    c = MIXER_WIDTH
    n_glu = c // tn
    last = n_glu - 1
    return pl.pallas_call(
        functools.partial(_conv_inproj_kernel, n_glu=n_glu),
        grid=(t // tm, n_glu + 1),
        in_specs=[pl.BlockSpec((tm, d), lambda i, j: (i, 0)),
                  pl.BlockSpec((1, d), lambda i, j: (0, 0)),
                  pl.BlockSpec((d, tn), lambda i, j: (0, jnp.minimum(j, last))),
                  pl.BlockSpec((d, tn), lambda i, j: (0, n_glu + jnp.minimum(j, last))),
                  pl.BlockSpec((d, XATTN_WIDTH), lambda i, j: (0, 2 * c // XATTN_WIDTH))],
        out_specs=[pl.BlockSpec((tm, tn), lambda i, j: (i, jnp.minimum(j, last))),
                   pl.BlockSpec((tm, XATTN_WIDTH), lambda i, j: (i, 0))],
        out_shape=[jax.ShapeDtypeStruct((t, c), F32),
                   jax.ShapeDtypeStruct((t, XATTN_WIDTH), BF16)],
        scratch_shapes=[pltpu.VMEM((tm, d), BF16)],
        compiler_params=_params(("arbitrary", "arbitrary")),
        name="conv_inproj",
    )(x, g, w, w, w)


def _conv_mixer_kernel(u_ref, hist_ref, w_ref, b_ref, lg_ref, lb_ref, o_ref, ubuf, cbuf, *, tt, tr, carry):
    i = pl.program_id(1)
    c = u_ref.shape[-1]

    @pl.when(i == 0)
    def _():
        ubuf[0:CONV_HIST, :] = hist_ref[0]

    ubuf[CONV_HIST:CONV_HIST + tt, :] = u_ref[0]

    off = CONV_HIST - CONV_STATE
    rows = tr + CONV_HIST
    for c0 in range(0, c, LANE):
        for r0 in range(0, tt, tr):
            window = ubuf[r0:r0 + rows, c0:c0 + LANE]
            acc = jnp.zeros((tr, LANE), F32) + b_ref[:, c0:c0 + LANE]
            for res in range(SUBLANE):
                shifted = pltpu.roll(window, rows - res, axis=0) if res else window
                for k in range(CONV_WIDTH):
                    if (off + k) % SUBLANE == res:
                        a0 = off + k - res
                        acc = acc + w_ref[k:k + 1, c0:c0 + LANE] * shifted[a0:a0 + tr]
            cbuf[r0:r0 + tr, c0:c0 + LANE] = acc

    y = cbuf[...]
    yc = y - jnp.mean(y, axis=-1, keepdims=True)
    yn = yc * lax.rsqrt(jnp.mean(yc * yc, axis=-1, keepdims=True) + EPS)
    yn = yn * lg_ref[...] + lb_ref[...]
    o_ref[0] = (yn * _sigmoid(yn)).astype(BF16)

    if carry:
        ubuf[0:CONV_HIST, :] = ubuf[tt:tt + CONV_HIST, :]


def conv_mixer(u, hist, conv_w, conv_b, ln_g, ln_b, tt):
    b, t, c = u.shape
    nblk = t // tt
    tr = min(tt, 64)
    return pl.pallas_call(
        functools.partial(_conv_mixer_kernel, tt=tt, tr=tr, carry=nblk > 1),
        grid=(b, nblk),
        in_specs=[pl.BlockSpec((1, tt, c), lambda bi, i: (bi, i, 0)),
                  pl.BlockSpec((1, CONV_HIST, c), lambda bi, i: (bi, 0, 0)),
                  pl.BlockSpec((CONV_WIDTH, c), lambda bi, i: (0, 0)),
                  pl.BlockSpec((1, c), lambda bi, i: (0, 0)),
                  pl.BlockSpec((1, c), lambda bi, i: (0, 0)),
                  pl.BlockSpec((1, c), lambda bi, i: (0, 0))],
        out_specs=pl.BlockSpec((1, tt, c), lambda bi, i: (bi, i, 0)),
        out_shape=jax.ShapeDtypeStruct((b, t, c), BF16),
        scratch_shapes=[pltpu.VMEM((CONV_HIST + tt, c), F32), pltpu.VMEM((tt, c), F32)],
        compiler_params=_params(("arbitrary", "arbitrary")),
        name="conv_mixer",
    )(u, hist, conv_w, conv_b, ln_g, ln_b)


def _fox_inproj_kernel(x_ref, g_ref, w_ref, wf_ref, bf_ref,
                       q_ref, kf_ref, kb_ref, vf_ref, vb_ref, qx_ref, lf_ref, xn_ref,
                       *, n_part, transposed, q_scale):
    j = pl.program_id(1)

    def oriented(a):
        return a.T if transposed else a

    @pl.when(j == 0)
    def _():
        xn = _rms(x_ref[...], g_ref[...]).astype(BF16)
        xn_ref[...] = xn
        f = _dot_nt(wf_ref[...], xn) + bf_ref[...]
        lf_ref[...] = jnp.minimum(f, 0.0) - jnp.log1p(jnp.exp(-jnp.abs(f)))

    z = _dot(xn_ref[...], w_ref[...])

    @pl.when(j < n_part)
    def _():
        q_ref[...] = oriented(z * q_scale).astype(BF16)

    def store_heads(ref):
        nseq, nheads, rows, _ = ref.shape
        for sq in range(nseq):
            for hh in range(nheads):
                ref[sq, hh] = z[sq * rows:(sq + 1) * rows, hh * CHUNK_HEAD:(hh + 1) * CHUNK_HEAD]

    @pl.when(jnp.logical_and(j >= n_part, j < 2 * n_part))
    def _():
        store_heads(kf_ref)
        kb_ref[...] = z.astype(BF16)

    @pl.when(jnp.logical_and(j >= 2 * n_part, j < 3 * n_part))
    def _():
        store_heads(vf_ref)
        vb_ref[...] = oriented(z).astype(BF16)

    @pl.when(j == 3 * n_part)
    def _():
        qx_ref[...] = z.astype(BF16)


def fox_inproj(x, g, w, wf_t, bf, nseq_total, tm, tn, transposed, q_scale):
    t, d = x.shape
    c = MIXER_WIDTH
    s = t // nseq_total
    n_part = c // tn
    assert tn == XATTN_WIDTH
    rows = min(tm, s)
    seq_per_blk = tm // rows
    blk_per_seq = s // rows

    def part(p):
        return lambda i, j: (i, jnp.clip(j - p * n_part, 0, n_part - 1))

    def part_t(p):
        return lambda i, j: (jnp.clip(j - p * n_part, 0, n_part - 1), i)

    def head_major(p):
        return pl.BlockSpec(
            (seq_per_blk, tn // CHUNK_HEAD, rows, CHUNK_HEAD),
            lambda i, j: (i // blk_per_seq, jnp.clip(j - p * n_part, 0, n_part - 1), i % blk_per_seq, 0))

    if transposed:
        qv_shape = jax.ShapeDtypeStruct((c, t), BF16)
        q_spec, v_spec = pl.BlockSpec((tn, tm), part_t(0)), pl.BlockSpec((tn, tm), part_t(2))
    else:
        qv_shape = jax.ShapeDtypeStruct((t, c), BF16)
        q_spec, v_spec = pl.BlockSpec((tm, tn), part(0)), pl.BlockSpec((tm, tn), part(2))
    kv_shape = jax.ShapeDtypeStruct((nseq_total, FOX_HEADS, s, CHUNK_HEAD), F32)

    return pl.pallas_call(
        functools.partial(_fox_inproj_kernel, n_part=n_part, transposed=transposed, q_scale=q_scale),
        grid=(t // tm, 3 * n_part + 1),
        in_specs=[pl.BlockSpec((tm, d), lambda i, j: (i, 0)),
                  pl.BlockSpec((1, d), lambda i, j: (0, 0)),
                  pl.BlockSpec((d, tn), lambda i, j: (0, j)),
                  pl.BlockSpec((LOGF_ROWS, d), lambda i, j: (0, 0)),
                  pl.BlockSpec((LOGF_ROWS, 1), lambda i, j: (0, 0))],
        out_specs=[q_spec,
                   head_major(1),
                   pl.BlockSpec((tm, tn), part(1)),
                   head_major(2),
                   v_spec,
                   pl.BlockSpec((tm, XATTN_WIDTH), lambda i, j: (i, 0)),
                   pl.BlockSpec((LOGF_ROWS, tm), lambda i, j: (0, i))],
        out_shape=[qv_shape,
                   kv_shape,
                   jax.ShapeDtypeStruct((t, c), BF16),
                   kv_shape,
                   qv_shape,
                   jax.ShapeDtypeStruct((t, XATTN_WIDTH), BF16),
                   jax.ShapeDtypeStruct((LOGF_ROWS, t), F32)],
        scratch_shapes=[pltpu.VMEM((tm, d), BF16)],
        compiler_params=_params(("arbitrary", "arbitrary")),
        name="fox_inproj",
    )(x, g, w, wf_t, bf)


def _neg_cumsum_kernel(x_ref, o_ref):
    x = x_ref[...]
    n = x.shape[-1]
    lane = lax.broadcasted_iota(jnp.int32, x.shape, x.ndim - 1)
    d = 1
    while d < n:
        x = x + jnp.where(lane >= d, pltpu.roll(x, d, axis=x.ndim - 1), 0.0)
        d *= 2
    o_ref[...] = -x


def neg_cumsum(x, nseq):
    r, total = x.shape
    seg = total // nseq
    return pl.pallas_call(
        _neg_cumsum_kernel,
        grid=(nseq,),
        in_specs=[pl.BlockSpec((r, seg), lambda b: (0, b))],
        out_specs=pl.BlockSpec((r, seg), lambda b: (0, b)),
        out_shape=jax.ShapeDtypeStruct((r, total), F32),
        compiler_params=_params(("arbitrary",)),
        name="neg_cumsum",
    )(x)


def _key_bias_kernel(nc_ref, o_ref):
    x = nc_ref[...] * LOG2E
    ts = x.shape[1]
    xt = jnp.concatenate([x, jnp.zeros((LANE - LOGF_ROWS, ts), F32)], axis=0).T
    lane = lax.broadcasted_iota(jnp.int32, (ts, LANE), 1)
    for h in range(FOX_HEADS):
        col = xt[:, h:h + 1]
        hi = col.astype(BF16).astype(F32)
        rest = col - hi
        mid = rest.astype(BF16).astype(F32)
        lo = rest - mid
        tile = jnp.where(lane == 0, hi, jnp.where(lane == 1, mid, jnp.where(lane == 2, lo, 0.0)))
        o_ref[0, :, h * CHUNK_HEAD:(h + 1) * CHUNK_HEAD] = tile.astype(BF16)


def key_bias(nc, b, s, ts):
    return pl.pallas_call(
        _key_bias_kernel,
        grid=(b, s // ts),
        in_specs=[pl.BlockSpec((LOGF_ROWS, ts), lambda bi, i: (0, bi * (s // ts) + i))],
        out_specs=pl.BlockSpec((1, ts, MIXER_WIDTH), lambda bi, i: (bi, i, 0)),
        out_shape=jax.ShapeDtypeStruct((b, s, MIXER_WIDTH), BF16),
        compiler_params=_params(("arbitrary", "arbitrary")),
        name="key_bias",
    )(nc)


def _fox_attn_kernel(qt_ref, k_ref, kb_ref, vt_ref, o_ref, qa_ref, sa_ref, sb_ref, m_ref, l_ref, acc_ref,
                     *, blk, hp):
    i = pl.program_id(2)
    heads = [slice(hh * CHUNK_HEAD, (hh + 1) * CHUNK_HEAD) for hh in range(hp)]

    ones = (lax.broadcasted_iota(jnp.int32, (CHUNK_HEAD, blk), 0) < BIAS_PARTS).astype(BF16)
    for hh, sl in enumerate(heads):
        qa_ref[hh, 0:CHUNK_HEAD, :] = qt_ref[sl, :]
        qa_ref[hh, CHUNK_HEAD:2 * CHUNK_HEAD, :] = ones
    m_ref[...] = jnp.full(m_ref.shape, -jnp.inf, F32)
    l_ref[...] = jnp.zeros(l_ref.shape, F32)
    acc_ref[...] = jnp.zeros(acc_ref.shape, F32)

    def scores(j, dst):
        start = pl.multiple_of(j * blk, blk)
        for hh, sl in enumerate(heads):
            ka = jnp.concatenate([k_ref[0, pl.ds(start, blk), sl], kb_ref[0, pl.ds(start, blk), sl]], axis=1)
            dst[hh] = _dot(ka, qa_ref[hh])

    def absorb(j, src, diagonal):
        start = pl.multiple_of(j * blk, blk)
        for hh, sl in enumerate(heads):
            s = src[hh]
            if diagonal:
                key = lax.broadcasted_iota(jnp.int32, s.shape, 0)
                qry = lax.broadcasted_iota(jnp.int32, s.shape, 1)
                s = jnp.where(key <= qry, s, -jnp.inf)
            m_prev = m_ref[hh]
            m_new = jnp.maximum(m_prev, jnp.max(s, axis=0, keepdims=True))
            alpha = jnp.exp2(m_prev - m_new)
            p = jnp.exp2(s - m_new)
            l_ref[hh] = alpha * l_ref[hh] + jnp.sum(p, axis=0, keepdims=True)
            acc_ref[hh] = alpha * acc_ref[hh] + _dot(vt_ref[sl, pl.ds(start, blk)], p.astype(BF16))
            m_ref[hh] = m_new

    scores(0, sa_ref)

    def pair(p, carry):
        j = 2 * p
        scores(j + 1, sb_ref)
        absorb(j, sa_ref, False)
        scores(j + 2, sa_ref)
        absorb(j + 1, sb_ref, False)
        return carry

    lax.fori_loop(0, jnp.right_shift(i, 1), pair, 0)

    @pl.when(jnp.bitwise_and(i, 1) == 0)
    def _():
        absorb(i, sa_ref, True)

    @pl.when(jnp.bitwise_and(i, 1) == 1)
    def _():
        scores(i, sb_ref)
        absorb(i - 1, sa_ref, False)
        absorb(i, sb_ref, True)

    for hh in range(hp):
        out_t = acc_ref[hh] * (1.0 / l_ref[hh])
        o_ref[0, :, hh * CHUNK_HEAD:(hh + 1) * CHUNK_HEAD] = out_t.T.astype(BF16)


def fox_attention(qt, k, kb, vt, blk, hp):
    b, s, w = k.shape
    h = w // CHUNK_HEAD
    wide = hp * CHUNK_HEAD
    nq = s // blk
    return pl.pallas_call(
        functools.partial(_fox_attn_kernel, blk=blk, hp=hp),
        grid=(b, h // hp, nq),
        in_specs=[pl.BlockSpec((wide, blk), lambda bi, hi, i: (hi, bi * nq + i)),
                  pl.BlockSpec((1, s, wide), lambda bi, hi, i: (bi, 0, hi)),
                  pl.BlockSpec((1, s, wide), lambda bi, hi, i: (bi, 0, hi)),
                  pl.BlockSpec((wide, s), lambda bi, hi, i: (hi, bi))],
        out_specs=pl.BlockSpec((1, blk, wide), lambda bi, hi, i: (bi, i, hi)),
        out_shape=jax.ShapeDtypeStruct((b, s, w), BF16),
        scratch_shapes=[pltpu.VMEM((hp, 2 * CHUNK_HEAD, blk), BF16),
                        pltpu.VMEM((hp, blk, blk), F32), pltpu.VMEM((hp, blk, blk), F32),
                        pltpu.VMEM((hp, 1, blk), F32), pltpu.VMEM((hp, 1, blk), F32),
                        pltpu.VMEM((hp, CHUNK_HEAD, blk), F32)],
        compiler_params=_params(("arbitrary", "arbitrary", "arbitrary")),
        name="fox_attention",
    )(qt, k, kb, vt)


def _fox_decode_kernel(q_ref, kc_ref, vc_ref, kn_ref, vn_ref, nc_ref, o_ref, *, past, t, heads):
    row = lax.broadcasted_iota(jnp.int32, (t, t), 0)
    col = lax.broadcasted_iota(jnp.int32, (t, t), 1)
    for h in range(heads):
        sl = slice(h * CHUNK_HEAD, (h + 1) * CHUNK_HEAD)
        q = q_ref[0, :, sl]
        s1 = _dot_nt(q, kc_ref[0, h].astype(BF16)) + nc_ref[h, :, 0:past]
        s2 = _dot_nt(q, kn_ref[0, :, sl]) + nc_ref[h, :, past:past + t]
        s2 = jnp.where(col <= row, s2, -jnp.inf)
        m = jnp.maximum(jnp.max(s1, axis=-1, keepdims=True), jnp.max(s2, axis=-1, keepdims=True))
        p1 = jnp.exp(s1 - m)
        p2 = jnp.exp(s2 - m)
        l = jnp.sum(p1, axis=-1, keepdims=True) + jnp.sum(p2, axis=-1, keepdims=True)
        o = _dot(p1.astype(BF16), vc_ref[0, h].astype(BF16)) + _dot(p2.astype(BF16), vn_ref[0, :, sl])
        o_ref[0, :, sl] = (o / l).astype(BF16)


def fox_decode_attention(q, kc, vc, kn, vn, nc):
    b, t, w = q.shape
    _, h, past, dh = kc.shape
    npad = nc.shape[-1]
    new = pl.BlockSpec((1, t, w), lambda bi: (bi, 0, 0))
    old = pl.BlockSpec((1, h, past, dh), lambda bi: (bi, 0, 0, 0))
    return pl.pallas_call(
        functools.partial(_fox_decode_kernel, past=past, t=t, heads=h),
        grid=(b,),
        in_specs=[new, old, old, new, new,
                  pl.BlockSpec((LOGF_ROWS, 1, npad), lambda bi: (bi, 0, 0))],
        out_specs=new,
        out_shape=jax.ShapeDtypeStruct((b, t, w), BF16),
        compiler_params=_params(("arbitrary",)),
        name="fox_decode_attention",
    )(q, kc, vc, kn, vn, nc)


def _mem_attn_kernel(q_ref, mk_ref, mv_ref, o_ref):
    for h in range(XATTN_HEADS):
        sl = slice(h * CHUNK_HEAD, (h + 1) * CHUNK_HEAD)
        s = _dot_nt(q_ref[0, :, sl], mk_ref[0, :, sl]) * ATTN_SCALE
        p = jnp.exp(s - jnp.max(s, axis=-1, keepdims=True))
        l = jnp.sum(p, axis=-1, keepdims=True)
        o = _dot(p.astype(BF16), mv_ref[0, :, sl])
        o_ref[0, :, sl] = (o / l).astype(BF16)


def mem_attention(qx, mk, mv, tm):
    b, t, w = qx.shape
    nm = mk.shape[1]
    return pl.pallas_call(
        _mem_attn_kernel,
        grid=(b, t // tm),
        in_specs=[pl.BlockSpec((1, tm, w), lambda bi, i: (bi, i, 0)),
                  pl.BlockSpec((1, nm, w), lambda bi, i: (bi, 0, 0)),
                  pl.BlockSpec((1, nm, w), lambda bi, i: (bi, 0, 0))],
        out_specs=pl.BlockSpec((1, tm, w), lambda bi, i: (bi, i, 0)),
        out_shape=jax.ShapeDtypeStruct((b, t, w), BF16),
        compiler_params=_params(("arbitrary", "arbitrary")),
        name="mem_attention",
    )(qx, mk, mv)


def _out_proj_kernel(x_ref, mix_ref, xo_ref, w1_ref, w2_ref, o_ref):
    o_ref[...] = x_ref[...] + _dot(mix_ref[...], w1_ref[...]) + _dot(xo_ref[...], w2_ref[...])


def out_proj(x, mix, xo, w, layer, tm):
    t, d = x.shape
    c = mix.shape[1]
    return pl.pallas_call(
        _out_proj_kernel,
        grid=(t // tm,),
        in_specs=[pl.BlockSpec((tm, d), lambda i: (i, 0)),
                  pl.BlockSpec((tm, c), lambda i: (i, 0)),
                  pl.BlockSpec((tm, XATTN_WIDTH), lambda i: (i, 0)),
                  pl.BlockSpec((None, c, d), lambda i: (layer, 0, 0)),
                  pl.BlockSpec((None, XATTN_WIDTH, d), lambda i: (layer, c // XATTN_WIDTH, 0))],
        out_specs=pl.BlockSpec((tm, d), lambda i: (i, 0)),
        out_shape=jax.ShapeDtypeStruct((t, d), F32),
        compiler_params=_params(("arbitrary",)),
        name="out_proj",
    )(x, mix, xo, w, w)


def _mlp_kernel(x_ref, g_ref, wu_ref, wd_ref, gf_ref, o_ref, xn_ref, *, final_norm):
    f = pl.program_id(1)

    @pl.when(f == 0)
    def _():
        x = x_ref[...]
        xn_ref[...] = _rms(x, g_ref[...]).astype(BF16)
        o_ref[...] = x

    h = jnp.maximum(_dot(xn_ref[...], wu_ref[...]), 0.0)
    o_ref[...] += _dot((h * h).astype(BF16), wd_ref[...])

    if final_norm:
        @pl.when(f == pl.num_programs(1) - 1)
        def _():
            o_ref[...] = _rms(o_ref[...], gf_ref[...])


def mlp(x, g, w_up, w_down, g_final, layer, tm, tf, final_norm):
    t, d = x.shape
    ff = w_up.shape[2]
    return pl.pallas_call(
        functools.partial(_mlp_kernel, final_norm=final_norm),
        grid=(t // tm, ff // tf),
        in_specs=[pl.BlockSpec((tm, d), lambda i, f: (i, 0)),
                  pl.BlockSpec((1, d), lambda i, f: (0, 0)),
                  pl.BlockSpec((None, d, tf), lambda i, f: (layer, 0, f)),
                  pl.BlockSpec((None, tf, d), lambda i, f: (layer, f, 0)),
                  pl.BlockSpec((1, d), lambda i, f: (0, 0))],
        out_specs=pl.BlockSpec((tm, d), lambda i, f: (i, 0)),
        out_shape=jax.ShapeDtypeStruct((t, d), F32),
        scratch_shapes=[pltpu.VMEM((tm, d), BF16)],
        compiler_params=_params(("arbitrary", "arbitrary")),
        name="mlp",
    )(x, g, w_up, w_down, g_final)


def _token_tile(t, pref):
    return pref if t % pref == 0 else t


def kernel(x_prompt, x_sample, cache_mem_k, cache_mem_v, state_conv, cache_fox_k, cache_fox_v, cache_fox_logf, mem_prompt, g_mix, g_mem, w_mem_k, w_mem_v, w_in_conv, conv_w, conv_b, conv_ln_g, conv_ln_b, w_in_fox, b_fox_f, w_out, g_mlp, w_up, w_down, g_final):
    bp, sp, d = x_prompt.shape
    bs, ss, _ = x_sample.shape
    depth = g_mix.shape[0]
    n_mem = mem_prompt.shape[1]
    past = cache_fox_k.shape[2]
    c = MIXER_WIDTH

    groups = {
        "p": dict(y=x_prompt.reshape(bp * sp, d), b=bp, s=sp, tm_conv=1024, tm_fox=1024, tconv=128),
        "s": dict(y=x_sample.reshape(bs * ss, d), b=bs, s=ss, tm_conv=bs * ss, tm_fox=bs * ss, tconv=ss),
    }
    mem_flat = mem_prompt.reshape(bp * n_mem, d)
    row = lambda a: a.reshape(1, -1)
    w_o, w_u, w_d = w_out.astype(BF16), w_up.astype(BF16), w_down.astype(BF16)

    mem_k_p, mem_v_p, conv_p, conv_s = [], [], [], []
    fk, fv, fl = {"p": [], "s": []}, {"p": [], "s": []}, {"p": [], "s": []}

    for i in range(depth):
        j = i // 2
        w_kv = jnp.concatenate([w_mem_k[i], w_mem_v[i]], axis=1).astype(BF16)
        kv = rms_matmul(mem_flat, row(g_mem[i]), w_kv, XATTN_WIDTH)
        mk_p = kv[:, :XATTN_WIDTH].reshape(bp, n_mem, XATTN_WIDTH)
        mv_p = kv[:, XATTN_WIDTH:].reshape(bp, n_mem, XATTN_WIDTH)
        mem_k_p.append(mk_p.reshape(bp, n_mem, XATTN_HEADS, CHUNK_HEAD))
        mem_v_p.append(mv_p.reshape(bp, n_mem, XATTN_HEADS, CHUNK_HEAD))
        mem = {"p": (mk_p.astype(BF16), mv_p.astype(BF16)),
               "s": (cache_mem_k[i].reshape(bs, n_mem, XATTN_WIDTH).astype(BF16),
                     cache_mem_v[i].reshape(bs, n_mem, XATTN_WIDTH).astype(BF16))}
        if i % 2 == 0:
            w_in = w_in_conv[j].astype(BF16)
        else:
            wf = w_in_fox[j]
            w_in = jnp.concatenate([wf[:, :3 * c], wf[:, 3 * c + FOX_HEADS:]], axis=1).astype(BF16)
            wf_t = jnp.zeros((LOGF_ROWS, d), BF16).at[:FOX_HEADS].set(wf[:, 3 * c:3 * c + FOX_HEADS].T.astype(BF16))
            bf = jnp.zeros((LOGF_ROWS, 1), F32).at[:FOX_HEADS, 0].set(b_fox_f[j])

        for name, grp in groups.items():
            y, b, s = grp["y"], grp["b"], grp["s"]
            if i % 2 == 0:
                u, qx = conv_inproj(y, row(g_mix[i]), w_in, grp["tm_conv"], XATTN_WIDTH)
                u = u.reshape(b, s, c)
                if name == "p":
                    hist = jnp.zeros((b, CONV_HIST, c), F32)
                    conv_p.append(u[:, s - CONV_STATE:])
                else:
                    st = state_conv[j]
                    hist = jnp.concatenate([jnp.zeros((b, CONV_HIST - CONV_STATE, c), F32), st], axis=1)
                    conv_s.append(jnp.concatenate([st, u], axis=1)[:, s:])
                mix = conv_mixer(u, hist, conv_w[j], row(conv_b[j]), row(conv_ln_g[j]), row(conv_ln_b[j]),
                                 grp["tconv"])
            else:
                q, k_f, k_b, v_f, v_b, qx, lf_t = fox_inproj(
                    y, row(g_mix[i]), w_in, wf_t, bf, b, grp["tm_fox"], XATTN_WIDTH, transposed=(name == "p"),
                    q_scale=ATTN_SCALE * LOG2E if name == "p" else ATTN_SCALE)
                fk[name].append(k_f.transpose(0, 2, 1, 3))
                fv[name].append(v_f.transpose(0, 2, 1, 3))
                fl[name].append(lf_t[:FOX_HEADS].reshape(FOX_HEADS, b, s).transpose(1, 2, 0))
                k_b = k_b.reshape(b, s, c)
                if name == "p":
                    nc = neg_cumsum(lf_t, b)
                    mix = fox_attention(q, k_b, key_bias(nc, b, s, 512), v_b, 512, 2)
                else:
                    q = q.reshape(b, s, c)
                    v_b = v_b.reshape(b, s, c)
                    total = past + s
                    padded = -(-total // LANE) * LANE
                    lf_old = jnp.zeros((b, LOGF_ROWS, past), F32).at[:, :FOX_HEADS].set(
                        cache_fox_logf[j].astype(F32).transpose(0, 2, 1))
                    lf_all = jnp.concatenate(
                        [lf_old, lf_t.reshape(LOGF_ROWS, b, s).transpose(1, 0, 2),
                         jnp.zeros((b, LOGF_ROWS, padded - total), F32)], axis=2)
                    lf_all = lf_all.transpose(1, 0, 2).reshape(LOGF_ROWS, b * padded)
                    nc = neg_cumsum(lf_all, b)
                    nc = nc.reshape(LOGF_ROWS, b, padded).transpose(1, 0, 2).reshape(b * LOGF_ROWS, 1, padded)
                    mix = fox_decode_attention(q, cache_fox_k[j].transpose(0, 2, 1, 3),
                                               cache_fox_v[j].transpose(0, 2, 1, 3), k_b, v_b, nc)
            qx = qx.reshape(b, s, XATTN_WIDTH)
            xo = mem_attention(qx, mem[name][0], mem[name][1], _token_tile(s, 512))
            y = out_proj(y, mix.reshape(b * s, c), xo.reshape(b * s, XATTN_WIDTH), w_o, i, _token_tile(b * s, 512))
            y = mlp(y, row(g_mlp[i]), w_u, w_d, row(g_final), i, _token_tile(b * s, 512), 1024,
                    final_norm=(i == depth - 1))
            grp["y"] = y

    y_prompt = groups["p"]["y"].reshape(bp, sp, d)
    y_sample = groups["s"]["y"].reshape(bs, ss, d)
    return (y_prompt, y_sample, jnp.stack(mem_k_p), jnp.stack(mem_v_p), jnp.stack(conv_p), jnp.stack(conv_s),
            jnp.stack(fk["p"]), jnp.stack(fv["p"]), jnp.stack(fl["p"]),
            jnp.stack(fk["s"]), jnp.stack(fv["s"]), jnp.stack(fl["s"]))
```

```python
import functools

import jax
import jax.numpy as jnp
from jax import lax
from jax.experimental import pallas as pl
from jax.experimental.pallas import tpu as pltpu

F32 = jnp.float32
BF16 = jnp.bfloat16

D_MODEL = 2048
CHUNK_HEAD = 128
XATTN_HEADS = 4
XATTN_WIDTH = XATTN_HEADS * CHUNK_HEAD
MIXER_WIDTH = D_MODEL - XATTN_WIDTH
FOX_HEADS = MIXER_WIDTH // CHUNK_HEAD
CONV_WIDTH = 31
CONV_STATE = CONV_WIDTH - 1
CONV_HIST = 32
D_FF = 4 * D_MODEL
EPS = 1e-6
ATTN_SCALE = CHUNK_HEAD ** -0.5
LOG2E = 1.4426950408889634
LOGF_ROWS = 16
LANE = 128
SUBLANE = 8
BIAS_PARTS = 3

VMEM_LIMIT = 56 * 1024 * 1024


def _params(sem):
    return pltpu.CompilerParams(dimension_semantics=sem, vmem_limit_bytes=VMEM_LIMIT)


def _rms(x, g):
    ms = jnp.mean(x * x, axis=-1, keepdims=True)
    return x * lax.rsqrt(ms + EPS) * g


def _sigmoid(x):
    return 1.0 / (1.0 + jnp.exp(-x))


def _dot(a, b):
    return jnp.dot(a, b, preferred_element_type=F32)


def _dot_nt(a, b):
    return lax.dot_general(a, b, (((1,), (1,)), ((), ())), preferred_element_type=F32)


def _rms_matmul_kernel(x_ref, g_ref, w_ref, o_ref):
    xn = _rms(x_ref[...], g_ref[...]).astype(BF16)
    o_ref[...] = _dot(xn, w_ref[...])


def rms_matmul(x, g, w, tn):
    t, d = x.shape
    n = w.shape[1]
    return pl.pallas_call(
        _rms_matmul_kernel,
        grid=(n // tn,),
        in_specs=[pl.BlockSpec((t, d), lambda j: (0, 0)),
                  pl.BlockSpec((1, d), lambda j: (0, 0)),
                  pl.BlockSpec((d, tn), lambda j: (0, j))],
        out_specs=pl.BlockSpec((t, tn), lambda j: (0, j)),
        out_shape=jax.ShapeDtypeStruct((t, n), F32),
        compiler_params=_params(("arbitrary",)),
        name="rms_matmul",
    )(x, g, w)


def _conv_front_kernel(x_ref, g_ref, wa_ref, wg_ref, wq_ref, hist_ref, cw_ref, cb_ref, lg_ref, lb_ref,
                       mix_ref, qx_ref, tail_ref, xn_ref, win_a, win_b, carry_ref, conv_ref,
                       *, n_glu, blk_per_seq, tr):
    i = pl.program_id(0)
    j = pl.program_id(1)
    nseq, _, tn = win_a.shape
    rows = win_a.shape[1] - CONV_HIST
    first_block = (i % blk_per_seq) == 0
    windows = (win_a, win_b)

    def project(p):
        xn = xn_ref[...]
        u = _dot(xn, wa_ref[...]) * _sigmoid(_dot(xn, wg_ref[...]))
        win = windows[p % 2]
        cols = slice(p * tn, (p + 1) * tn)
        for sq in range(nseq):
            win[sq, 0:CONV_HIST, :] = jnp.where(first_block, hist_ref[sq, :, cols], carry_ref[sq, :, cols])
            win[sq, CONV_HIST:CONV_HIST + rows, :] = u[sq * rows:(sq + 1) * rows]

    def convolve(p):
        win = windows[p % 2]
        off = CONV_HIST - CONV_STATE
        span = tr + CONV_HIST
        for sq in range(nseq):
            for c0 in range(0, tn, LANE):
                ch = slice(p * tn + c0, p * tn + c0 + LANE)
                for r0 in range(0, rows, tr):
                    window = win[sq, r0:r0 + span, c0:c0 + LANE]
                    acc = jnp.zeros((tr, LANE), F32) + cb_ref[:, ch]
                    for res in range(SUBLANE):
                        shifted = pltpu.roll(window, span - res, axis=0) if res else window
                        for k in range(CONV_WIDTH):
                            if (off + k) % SUBLANE == res:
                                a0 = off + k - res
                                acc = acc + cw_ref[k:k + 1, ch] * shifted[a0:a0 + tr]
                    conv_ref[sq * rows + r0:sq * rows + r0 + tr, ch] = acc
            tail = win[sq, rows:rows + CONV_HIST, :]
            carry_ref[sq, :, p * tn:(p + 1) * tn] = tail
            tail_ref[sq, :, p * tn:(p + 1) * tn] = tail

    @pl.when(jnp.logical_and(i == 0, j == 0))
    def _():
        carry_ref[...] = jnp.zeros(carry_ref.shape, F32)

    @pl.when(j == 0)
    def _():
        xn_ref[...] = _rms(x_ref[...], g_ref[...]).astype(BF16)
        project(0)

    for p in range(1, n_glu):
        @pl.when(j == p)
        def _():
            convolve(p - 1)
            project(p)

    @pl.when(j == n_glu)
    def _():
        convolve(n_glu - 1)
        qx_ref[...] = _dot(xn_ref[...], wq_ref[...]).astype(BF16)
        y = conv_ref[...]
        yc = y - jnp.mean(y, axis=-1, keepdims=True)
        yn = yc * lax.rsqrt(jnp.mean(yc * yc, axis=-1, keepdims=True) + EPS)
        yn = yn * lg_ref[...] + lb_ref[...]
        mix_ref[...] = (yn * _sigmoid(yn)).astype(BF16)


def conv_front(x, g, w, hist, conv_w, conv_b, ln_g, ln_b, tm, tn):
    t, d = x.shape
    nseq_total, _, c = hist.shape
    s = t // nseq_total
    n_glu = c // tn
    last = n_glu - 1
    rows = min(tm, s)
    seq_per_blk = tm // rows
    blk_per_seq = s // rows
    per_seq = pl.BlockSpec((seq_per_blk, CONV_HIST, c), lambda i, j: (i // blk_per_seq, 0, 0))
    const = lambda shape: pl.BlockSpec(shape, lambda i, j: (0, 0))
    return pl.pallas_call(
        functools.partial(_conv_front_kernel, n_glu=n_glu, blk_per_seq=blk_per_seq, tr=min(rows, 64)),
        grid=(t // tm, n_glu + 1),
        in_specs=[pl.BlockSpec((tm, d), lambda i, j: (i, 0)),
                  const((1, d)),
                  pl.BlockSpec((d, tn), lambda i, j: (0, jnp.minimum(j, last))),
                  pl.BlockSpec((d, tn), lambda i, j: (0, n_glu + jnp.minimum(j, last))),
                  pl.BlockSpec((d, XATTN_WIDTH), lambda i, j: (0, 2 * c // XATTN_WIDTH)),
                  per_seq,
                  const((CONV_WIDTH, c)), const((1, c)), const((1, c)), const((1, c))],
        out_specs=[pl.BlockSpec((tm, c), lambda i, j: (i, 0)),
                   pl.BlockSpec((tm, XATTN_WIDTH), lambda i, j: (i, 0)),
                   per_seq],
        out_shape=[jax.ShapeDtypeStruct((t, c), BF16),
                   jax.ShapeDtypeStruct((t, XATTN_WIDTH), BF16),
                   jax.ShapeDtypeStruct((nseq_total, CONV_HIST, c), F32)],
        scratch_shapes=[pltpu.VMEM((tm, d), BF16),
                        pltpu.VMEM((seq_per_blk, CONV_HIST + rows, tn), F32),
                        pltpu.VMEM((seq_per_blk, CONV_HIST + rows, tn), F32),
                        pltpu.VMEM((seq_per_blk, CONV_HIST, c), F32),
                        pltpu.VMEM((tm, c), F32)],
        compiler_params=_params(("arbitrary", "arbitrary")),
        name="conv_front",
    )(x, g, w, w, w, hist, conv_w, conv_b, ln_g, ln_b)


def _fox_inproj_kernel(x_ref, g_ref, w_ref, wf_ref, bf_ref,
                       q_ref, kf_ref, kb_ref, vf_ref, vb_ref, qx_ref, lf_ref, xn_ref,
                       *, n_part, transposed, q_scale):
    j = pl.program_id(1)

    def oriented(a):
        return a.T if transposed else a

    @pl.when(j == 0)
    def _():
        xn = _rms(x_ref[...], g_ref[...]).astype(BF16)
        xn_ref[...] = xn
        f = _dot_nt(wf_ref[...], xn) + bf_ref[...]
        lf_ref[...] = jnp.minimum(f, 0.0) - jnp.log1p(jnp.exp(-jnp.abs(f)))

    z = _dot(xn_ref[...], w_ref[...])

    @pl.when(j < n_part)
    def _():
        q_ref[...] = oriented(z * q_scale).astype(BF16)

    def store_heads(ref):
        nseq, nheads, rows, _ = ref.shape
        for sq in range(nseq):
            for hh in range(nheads):
                ref[sq, hh] = z[sq * rows:(sq + 1) * rows, hh * CHUNK_HEAD:(hh + 1) * CHUNK_HEAD]

    @pl.when(jnp.logical_and(j >= n_part, j < 2 * n_part))
    def _():
        store_heads(kf_ref)
        kb_ref[...] = z.astype(BF16)

    @pl.when(jnp.logical_and(j >= 2 * n_part, j < 3 * n_part))
    def _():
        store_heads(vf_ref)
        vb_ref[...] = oriented(z).astype(BF16)

    @pl.when(j == 3 * n_part)
    def _():
        qx_ref[...] = z.astype(BF16)


def fox_inproj(x, g, w, wf_t, bf, nseq_total, tm, tn, transposed, q_scale):
    t, d = x.shape
    c = MIXER_WIDTH
    s = t // nseq_total
    n_part = c // tn
    assert tn == XATTN_WIDTH
    rows = min(tm, s)
    seq_per_blk = tm // rows
    blk_per_seq = s // rows

    def part(p):
        return lambda i, j: (i, jnp.clip(j - p * n_part, 0, n_part - 1))

    def part_t(p):
        return lambda i, j: (jnp.clip(j - p * n_part, 0, n_part - 1), i)

    def head_major(p):
        return pl.BlockSpec(
            (seq_per_blk, tn // CHUNK_HEAD, rows, CHUNK_HEAD),
            lambda i, j: (i // blk_per_seq, jnp.clip(j - p * n_part, 0, n_part - 1), i % blk_per_seq, 0))

    if transposed:
        qv_shape = jax.ShapeDtypeStruct((c, t), BF16)
        q_spec, v_spec = pl.BlockSpec((tn, tm), part_t(0)), pl.BlockSpec((tn, tm), part_t(2))
    else:
        qv_shape = jax.ShapeDtypeStruct((t, c), BF16)
        q_spec, v_spec = pl.BlockSpec((tm, tn), part(0)), pl.BlockSpec((tm, tn), part(2))
    kv_shape = jax.ShapeDtypeStruct((nseq_total, FOX_HEADS, s, CHUNK_HEAD), F32)

    return pl.pallas_call(
        functools.partial(_fox_inproj_kernel, n_part=n_part, transposed=transposed, q_scale=q_scale),
        grid=(t // tm, 3 * n_part + 1),
        in_specs=[pl.BlockSpec((tm, d), lambda i, j: (i, 0)),
                  pl.BlockSpec((1, d), lambda i, j: (0, 0)),
                  pl.BlockSpec((d, tn), lambda i, j: (0, j)),
                  pl.BlockSpec((LOGF_ROWS, d), lambda i, j: (0, 0)),
                  pl.BlockSpec((LOGF_ROWS, 1), lambda i, j: (0, 0))],
        out_specs=[q_spec,
                   head_major(1),
                   pl.BlockSpec((tm, tn), part(1)),
                   head_major(2),
                   v_spec,
                   pl.BlockSpec((tm, XATTN_WIDTH), lambda i, j: (i, 0)),
                   pl.BlockSpec((LOGF_ROWS, tm), lambda i, j: (0, i))],
        out_shape=[qv_shape,
                   kv_shape,
                   jax.ShapeDtypeStruct((t, c), BF16),
                   kv_shape,
                   qv_shape,
                   jax.ShapeDtypeStruct((t, XATTN_WIDTH), BF16),
                   jax.ShapeDtypeStruct((LOGF_ROWS, t), F32)],
        scratch_shapes=[pltpu.VMEM((tm, d), BF16)],
        compiler_params=_params(("arbitrary", "arbitrary")),
        name="fox_inproj",
    )(x, g, w, wf_t, bf)


def _neg_cumsum_kernel(x_ref, o_ref):
    x = x_ref[...]
    n = x.shape[-1]
    lane = lax.broadcasted_iota(jnp.int32, x.shape, x.ndim - 1)
    d = 1
    while d < n:
        x = x + jnp.where(lane >= d, pltpu.roll(x, d, axis=x.ndim - 1), 0.0)
        d *= 2
    o_ref[...] = -x


def neg_cumsum(x, nseq):
    r, total = x.shape
    seg = total // nseq
    return pl.pallas_call(
        _neg_cumsum_kernel,
        grid=(nseq,),
        in_specs=[pl.BlockSpec((r, seg), lambda b: (0, b))],
        out_specs=pl.BlockSpec((r, seg), lambda b: (0, b)),
        out_shape=jax.ShapeDtypeStruct((r, total), F32),
        compiler_params=_params(("arbitrary",)),
        name="neg_cumsum",
    )(x)


def _key_bias_kernel(nc_ref, o_ref):
    x = nc_ref[...] * LOG2E
    ts = x.shape[1]
    xt = jnp.concatenate([x, jnp.zeros((LANE - LOGF_ROWS, ts), F32)], axis=0).T
    lane = lax.broadcasted_iota(jnp.int32, (ts, LANE), 1)
    for h in range(FOX_HEADS):
        col = xt[:, h:h + 1]
        hi = col.astype(BF16).astype(F32)
        rest = col - hi
        mid = rest.astype(BF16).astype(F32)
        lo = rest - mid
        tile = jnp.where(lane == 0, hi, jnp.where(lane == 1, mid, jnp.where(lane == 2, lo, 0.0)))
        o_ref[0, :, h * CHUNK_HEAD:(h + 1) * CHUNK_HEAD] = tile.astype(BF16)


def key_bias(nc, b, s, ts):
    return pl.pallas_call(
        _key_bias_kernel,
        grid=(b, s // ts),
        in_specs=[pl.BlockSpec((LOGF_ROWS, ts), lambda bi, i: (0, bi * (s // ts) + i))],
        out_specs=pl.BlockSpec((1, ts, MIXER_WIDTH), lambda bi, i: (bi, i, 0)),
        out_shape=jax.ShapeDtypeStruct((b, s, MIXER_WIDTH), BF16),
        compiler_params=_params(("arbitrary", "arbitrary")),
        name="key_bias",
    )(nc)


def _fox_attn_kernel(qt_ref, k_ref, kb_ref, vt_ref, o_ref, qa_ref, sa_ref, sb_ref, m_ref, l_ref, acc_ref,
                     *, blk, hp):
    i = pl.program_id(2)
    heads = [slice(hh * CHUNK_HEAD, (hh + 1) * CHUNK_HEAD) for hh in range(hp)]

    ones = (lax.broadcasted_iota(jnp.int32, (CHUNK_HEAD, blk), 0) < BIAS_PARTS).astype(BF16)
    for hh, sl in enumerate(heads):
        qa_ref[hh, 0:CHUNK_HEAD, :] = qt_ref[sl, :]
        qa_ref[hh, CHUNK_HEAD:2 * CHUNK_HEAD, :] = ones
    m_ref[...] = jnp.full(m_ref.shape, -jnp.inf, F32)
    l_ref[...] = jnp.zeros(l_ref.shape, F32)
    acc_ref[...] = jnp.zeros(acc_ref.shape, F32)

    def scores(j, dst):
        start = pl.multiple_of(j * blk, blk)
        for hh, sl in enumerate(heads):
            ka = jnp.concatenate([k_ref[0, pl.ds(start, blk), sl], kb_ref[0, pl.ds(start, blk), sl]], axis=1)
            dst[hh] = _dot(ka, qa_ref[hh])

    def absorb(j, src, diagonal):
        start = pl.multiple_of(j * blk, blk)
        for hh, sl in enumerate(heads):
            s = src[hh]
            if diagonal:
                key = lax.broadcasted_iota(jnp.int32, s.shape, 0)
                qry = lax.broadcasted_iota(jnp.int32, s.shape, 1)
                s = jnp.where(key <= qry, s, -jnp.inf)
            m_prev = m_ref[hh]
            m_new = jnp.maximum(m_prev, jnp.max(s, axis=0, keepdims=True))
            alpha = jnp.exp2(m_prev - m_new)
            p = jnp.exp2(s - m_new)
            l_ref[hh] = alpha * l_ref[hh] + jnp.sum(p, axis=0, keepdims=True)
            acc_ref[hh] = alpha * acc_ref[hh] + _dot(vt_ref[sl, pl.ds(start, blk)], p.astype(BF16))
            m_ref[hh] = m_new

    scores(0, sa_ref)

    def pair(p, carry):
        j = 2 * p
        scores(j + 1, sb_ref)
        absorb(j, sa_ref, False)
        scores(j + 2, sa_ref)
        absorb(j + 1, sb_ref, False)
        return carry

    lax.fori_loop(0, jnp.right_shift(i, 1), pair, 0)

    @pl.when(jnp.bitwise_and(i, 1) == 0)
    def _():
        absorb(i, sa_ref, True)

    @pl.when(jnp.bitwise_and(i, 1) == 1)
    def _():
        scores(i, sb_ref)
        absorb(i - 1, sa_ref, False)
        absorb(i, sb_ref, True)

    for hh in range(hp):
        out_t = acc_ref[hh] * (1.0 / l_ref[hh])
        o_ref[0, :, hh * CHUNK_HEAD:(hh + 1) * CHUNK_HEAD] = out_t.T.astype(BF16)


def fox_attention(qt, k, kb, vt, blk, hp):
    b, s, w = k.shape
    h = w // CHUNK_HEAD
    wide = hp * CHUNK_HEAD
    nq = s // blk
    return pl.pallas_call(
        functools.partial(_fox_attn_kernel, blk=blk, hp=hp),
        grid=(b, h // hp, nq),
        in_specs=[pl.BlockSpec((wide, blk), lambda bi, hi, i: (hi, bi * nq + i)),
                  pl.BlockSpec((1, s, wide), lambda bi, hi, i: (bi, 0, hi)),
                  pl.BlockSpec((1, s, wide), lambda bi, hi, i: (bi, 0, hi)),
                  pl.BlockSpec((wide, s), lambda bi, hi, i: (hi, bi))],
        out_specs=pl.BlockSpec((1, blk, wide), lambda bi, hi, i: (bi, i, hi)),
        out_shape=jax.ShapeDtypeStruct((b, s, w), BF16),
        scratch_shapes=[pltpu.VMEM((hp, 2 * CHUNK_HEAD, blk), BF16),
                        pltpu.VMEM((hp, blk, blk), F32), pltpu.VMEM((hp, blk, blk), F32),
                        pltpu.VMEM((hp, 1, blk), F32), pltpu.VMEM((hp, 1, blk), F32),
                        pltpu.VMEM((hp, CHUNK_HEAD, blk), F32)],
        compiler_params=_params(("arbitrary", "arbitrary", "arbitrary")),
        name="fox_attention",
    )(qt, k, kb, vt)


def _fox_decode_kernel(q_ref, kc_ref, vc_ref, kn_ref, vn_ref, nc_ref, o_ref, *, past, t, heads):
    row = lax.broadcasted_iota(jnp.int32, (t, t), 0)
    col = lax.broadcasted_iota(jnp.int32, (t, t), 1)
    for h in range(heads):
        sl = slice(h * CHUNK_HEAD, (h + 1) * CHUNK_HEAD)
        q = q_ref[0, :, sl]
        s1 = _dot_nt(q, kc_ref[0, h].astype(BF16)) + nc_ref[h, :, 0:past]
        s2 = _dot_nt(q, kn_ref[0, :, sl]) + nc_ref[h, :, past:past + t]
        s2 = jnp.where(col <= row, s2, -jnp.inf)
        m = jnp.maximum(jnp.max(s1, axis=-1, keepdims=True), jnp.max(s2, axis=-1, keepdims=True))
        p1 = jnp.exp(s1 - m)
        p2 = jnp.exp(s2 - m)
        l = jnp.sum(p1, axis=-1, keepdims=True) + jnp.sum(p2, axis=-1, keepdims=True)
        o = _dot(p1.astype(BF16), vc_ref[0, h].astype(BF16)) + _dot(p2.astype(BF16), vn_ref[0, :, sl])
        o_ref[0, :, sl] = (o / l).astype(BF16)


def fox_decode_attention(q, kc, vc, kn, vn, nc):
    b, t, w = q.shape
    _, h, past, dh = kc.shape
    npad = nc.shape[-1]
    new = pl.BlockSpec((1, t, w), lambda bi: (bi, 0, 0))
    old = pl.BlockSpec((1, h, past, dh), lambda bi: (bi, 0, 0, 0))
    return pl.pallas_call(
        functools.partial(_fox_decode_kernel, past=past, t=t, heads=h),
        grid=(b,),
        in_specs=[new, old, old, new, new,
                  pl.BlockSpec((LOGF_ROWS, 1, npad), lambda bi: (bi, 0, 0))],
        out_specs=new,
        out_shape=jax.ShapeDtypeStruct((b, t, w), BF16),
        compiler_params=_params(("arbitrary",)),
        name="fox_decode_attention",
    )(q, kc, vc, kn, vn, nc)


def _mem_attn_kernel(q_ref, mk_ref, mv_ref, o_ref):
    for h in range(XATTN_HEADS):
        sl = slice(h * CHUNK_HEAD, (h + 1) * CHUNK_HEAD)
        s = _dot_nt(q_ref[0, :, sl], mk_ref[0, :, sl]) * ATTN_SCALE
        p = jnp.exp(s - jnp.max(s, axis=-1, keepdims=True))
        l = jnp.sum(p, axis=-1, keepdims=True)
        o = _dot(p.astype(BF16), mv_ref[0, :, sl])
        o_ref[0, :, sl] = (o / l).astype(BF16)


def mem_attention(qx, mk, mv, tm):
    b, t, w = qx.shape
    nm = mk.shape[1]
    return pl.pallas_call(
        _mem_attn_kernel,
        grid=(b, t // tm),
        in_specs=[pl.BlockSpec((1, tm, w), lambda bi, i: (bi, i, 0)),
                  pl.BlockSpec((1, nm, w), lambda bi, i: (bi, 0, 0)),
                  pl.BlockSpec((1, nm, w), lambda bi, i: (bi, 0, 0))],
        out_specs=pl.BlockSpec((1, tm, w), lambda bi, i: (bi, i, 0)),
        out_shape=jax.ShapeDtypeStruct((b, t, w), BF16),
        compiler_params=_params(("arbitrary", "arbitrary")),
        name="mem_attention",
    )(qx, mk, mv)


def _out_proj_kernel(x_ref, mix_ref, xo_ref, w1_ref, w2_ref, o_ref):
    o_ref[...] = x_ref[...] + _dot(mix_ref[...], w1_ref[...]) + _dot(xo_ref[...], w2_ref[...])


def out_proj(x, mix, xo, w, layer, tm):
    t, d = x.shape
    c = mix.shape[1]
    return pl.pallas_call(
        _out_proj_kernel,
        grid=(t // tm,),
        in_specs=[pl.BlockSpec((tm, d), lambda i: (i, 0)),
                  pl.BlockSpec((tm, c), lambda i: (i, 0)),
                  pl.BlockSpec((tm, XATTN_WIDTH), lambda i: (i, 0)),
                  pl.BlockSpec((None, c, d), lambda i: (layer, 0, 0)),
                  pl.BlockSpec((None, XATTN_WIDTH, d), lambda i: (layer, c // XATTN_WIDTH, 0))],
        out_specs=pl.BlockSpec((tm, d), lambda i: (i, 0)),
        out_shape=jax.ShapeDtypeStruct((t, d), F32),
        compiler_params=_params(("arbitrary",)),
        name="out_proj",
    )(x, mix, xo, w, w)


def _mlp_kernel(x_ref, g_ref, wu_ref, wd_ref, gf_ref, o_ref, xn_ref, *, final_norm):
    f = pl.program_id(1)

    @pl.when(f == 0)
    def _():
        x = x_ref[...]
        xn_ref[...] = _rms(x, g_ref[...]).astype(BF16)
        o_ref[...] = x

    h = jnp.maximum(_dot(xn_ref[...], wu_ref[...]), 0.0)
    o_ref[...] += _dot((h * h).astype(BF16), wd_ref[...])

    if final_norm:
        @pl.when(f == pl.num_programs(1) - 1)
        def _():
            o_ref[...] = _rms(o_ref[...], gf_ref[...])


def mlp(x, g, w_up, w_down, g_final, layer, tm, tf, final_norm):
    t, d = x.shape
    ff = w_up.shape[2]
    return pl.pallas_call(
        functools.partial(_mlp_kernel, final_norm=final_norm),
        grid=(t // tm, ff // tf),
        in_specs=[pl.BlockSpec((tm, d), lambda i, f: (i, 0)),
                  pl.BlockSpec((1, d), lambda i, f: (0, 0)),
                  pl.BlockSpec((None, d, tf), lambda i, f: (layer, 0, f)),
                  pl.BlockSpec((None, tf, d), lambda i, f: (layer, f, 0)),
                  pl.BlockSpec((1, d), lambda i, f: (0, 0))],
        out_specs=pl.BlockSpec((tm, d), lambda i, f: (i, 0)),
        out_shape=jax.ShapeDtypeStruct((t, d), F32),
        scratch_shapes=[pltpu.VMEM((tm, d), BF16)],
        compiler_params=_params(("arbitrary", "arbitrary")),
        name="mlp",
    )(x, g, w_up, w_down, g_final)


def _token_tile(t, pref):
    return pref if t % pref == 0 else t


def kernel(x_prompt, x_sample, cache_mem_k, cache_mem_v, state_conv, cache_fox_k, cache_fox_v, cache_fox_logf, mem_prompt, g_mix, g_mem, w_mem_k, w_mem_v, w_in_conv, conv_w, conv_b, conv_ln_g, conv_ln_b, w_in_fox, b_fox_f, w_out, g_mlp, w_up, w_down, g_final):
    bp, sp, d = x_prompt.shape
    bs, ss, _ = x_sample.shape
    depth = g_mix.shape[0]
    n_mem = mem_prompt.shape[1]
    past = cache_fox_k.shape[2]
    c = MIXER_WIDTH

    groups = {
        "p": dict(y=x_prompt.reshape(bp * sp, d), b=bp, s=sp, tm_conv=512, tm_fox=1024),
        "s": dict(y=x_sample.reshape(bs * ss, d), b=bs, s=ss, tm_conv=bs * ss, tm_fox=bs * ss),
    }
    mem_flat = mem_prompt.reshape(bp * n_mem, d)
    row = lambda a: a.reshape(1, -1)
    w_o, w_u, w_d = w_out.astype(BF16), w_up.astype(BF16), w_down.astype(BF16)

    mem_k_p, mem_v_p, conv_p, conv_s = [], [], [], []
    fk, fv, fl = {"p": [], "s": []}, {"p": [], "s": []}, {"p": [], "s": []}

    for i in range(depth):
        j = i // 2
        w_kv = jnp.concatenate([w_mem_k[i], w_mem_v[i]], axis=1).astype(BF16)
        kv = rms_matmul(mem_flat, row(g_mem[i]), w_kv, XATTN_WIDTH)
        mk_p = kv[:, :XATTN_WIDTH].reshape(bp, n_mem, XATTN_WIDTH)
        mv_p = kv[:, XATTN_WIDTH:].reshape(bp, n_mem, XATTN_WIDTH)
        mem_k_p.append(mk_p.reshape(bp, n_mem, XATTN_HEADS, CHUNK_HEAD))
        mem_v_p.append(mv_p.reshape(bp, n_mem, XATTN_HEADS, CHUNK_HEAD))
        mem = {"p": (mk_p.astype(BF16), mv_p.astype(BF16)),
               "s": (cache_mem_k[i].reshape(bs, n_mem, XATTN_WIDTH).astype(BF16),
                     cache_mem_v[i].reshape(bs, n_mem, XATTN_WIDTH).astype(BF16))}
        if i % 2 == 0:
            w_in = w_in_conv[j].astype(BF16)
        else:
            wf = w_in_fox[j]
            w_in = jnp.concatenate([wf[:, :3 * c], wf[:, 3 * c + FOX_HEADS:]], axis=1).astype(BF16)
            wf_t = jnp.zeros((LOGF_ROWS, d), BF16).at[:FOX_HEADS].set(wf[:, 3 * c:3 * c + FOX_HEADS].T.astype(BF16))
            bf = jnp.zeros((LOGF_ROWS, 1), F32).at[:FOX_HEADS, 0].set(b_fox_f[j])

        for name, grp in groups.items():
            y, b, s = grp["y"], grp["b"], grp["s"]
            if i % 2 == 0:
                if name == "p":
                    hist = jnp.zeros((b, CONV_HIST, c), F32)
                else:
                    hist = jnp.concatenate([jnp.zeros((b, CONV_HIST - CONV_STATE, c), F32), state_conv[j]], axis=1)
                mix, qx, tail = conv_front(y, row(g_mix[i]), w_in, hist, conv_w[j], row(conv_b[j]),
                                           row(conv_ln_g[j]), row(conv_ln_b[j]), grp["tm_conv"], XATTN_WIDTH)
                (conv_p if name == "p" else conv_s).append(tail[:, CONV_HIST - CONV_STATE:])
            else:
                q, k_f, k_b, v_f, v_b, qx, lf_t = fox_inproj(
                    y, row(g_mix[i]), w_in, wf_t, bf, b, grp["tm_fox"], XATTN_WIDTH, transposed=(name == "p"),
                    q_scale=ATTN_SCALE * LOG2E if name == "p" else ATTN_SCALE)
                fk[name].append(k_f.transpose(0, 2, 1, 3))
                fv[name].append(v_f.transpose(0, 2, 1, 3))
                fl[name].append(lf_t[:FOX_HEADS].reshape(FOX_HEADS, b, s).transpose(1, 2, 0))
                k_b = k_b.reshape(b, s, c)
                if name == "p":
                    nc = neg_cumsum(lf_t, b)
                    mix = fox_attention(q, k_b, key_bias(nc, b, s, 512), v_b, 512, 2)
                else:
                    q = q.reshape(b, s, c)
                    v_b = v_b.reshape(b, s, c)
                    total = past + s
                    padded = -(-total // LANE) * LANE
                    lf_old = jnp.zeros((b, LOGF_ROWS, past), F32).at[:, :FOX_HEADS].set(
                        cache_fox_logf[j].astype(F32).transpose(0, 2, 1))
                    lf_all = jnp.concatenate(
                        [lf_old, lf_t.reshape(LOGF_ROWS, b, s).transpose(1, 0, 2),
                         jnp.zeros((b, LOGF_ROWS, padded - total), F32)], axis=2)
                    lf_all = lf_all.transpose(1, 0, 2).reshape(LOGF_ROWS, b * padded)
                    nc = neg_cumsum(lf_all, b)
                    nc = nc.reshape(LOGF_ROWS, b, padded).transpose(1, 0, 2).reshape(b * LOGF_ROWS, 1, padded)
                    mix = fox_decode_attention(q, cache_fox_k[j].transpose(0, 2, 1, 3),
                                               cache_fox_v[j].transpose(0, 2, 1, 3), k_b, v_b, nc)
            qx = qx.reshape(b, s, XATTN_WIDTH)
            xo = mem_attention(qx, mem[name][0], mem[name][1], _token_tile(s, 512))
            y = out_proj(y, mix.reshape(b * s, c), xo.reshape(b * s, XATTN_WIDTH), w_o, i, _token_tile(b * s, 512))
            y = mlp(y, row(g_mlp[i]), w_u, w_d, row(g_final), i, _token_tile(b * s, 512), 1024,
                    final_norm=(i == depth - 1))
            grp["y"] = y

    y_prompt = groups["p"]["y"].reshape(bp, sp, d)
    y_sample = groups["s"]["y"].reshape(bs, ss, d)
    return (y_prompt, y_sample, jnp.stack(mem_k_p), jnp.stack(mem_v_p), jnp.stack(conv_p), jnp.stack(conv_s),
            jnp.stack(fk["p"]), jnp.stack(fv["p"]), jnp.stack(fl["p"]),
            jnp.stack(fk["s"]), jnp.stack(fv["s"]), jnp.stack(fl["s"]))
```

```python
import functools

import jax
import jax.numpy as jnp
from jax import lax
from jax.experimental import pallas as pl
from jax.experimental.pallas import tpu as pltpu

F32 = jnp.float32
BF16 = jnp.bfloat16

D_MODEL = 2048
CHUNK_HEAD = 128
XATTN_HEADS = 4
XATTN_WIDTH = XATTN_HEADS * CHUNK_HEAD
MIXER_WIDTH = D_MODEL - XATTN_WIDTH
FOX_HEADS = MIXER_WIDTH // CHUNK_HEAD
CONV_WIDTH = 31
CONV_STATE = CONV_WIDTH - 1
CONV_HIST = 32
CONV_ROW_CHUNK = 128
D_FF = 4 * D_MODEL
EPS = 1e-6
ATTN_SCALE = CHUNK_HEAD ** -0.5
LOG2E = 1.4426950408889634
LOGF_ROWS = 16
LANE = 128
SUBLANE = 8
BIAS_PARTS = 3

VMEM_LIMIT = 56 * 1024 * 1024


def _params(sem):
    return pltpu.CompilerParams(dimension_semantics=sem, vmem_limit_bytes=VMEM_LIMIT)


def _rms(x, g):
    ms = jnp.mean(x * x, axis=-1, keepdims=True)
    return x * lax.rsqrt(ms + EPS) * g


def _sigmoid(x):
    return 1.0 / (1.0 + jnp.exp(-x))


def _dot(a, b):
    return jnp.dot(a, b, preferred_element_type=F32)


def _dot_nt(a, b):
    return lax.dot_general(a, b, (((1,), (1,)), ((), ())), preferred_element_type=F32)


def _rms_matmul_kernel(x_ref, g_ref, w_ref, o_ref):
    xn = _rms(x_ref[...], g_ref[...]).astype(BF16)
    o_ref[...] = _dot(xn, w_ref[...])


def rms_matmul(x, g, w, tn):
    t, d = x.shape
    n = w.shape[1]
    return pl.pallas_call(
        _rms_matmul_kernel,
        grid=(n // tn,),
        in_specs=[pl.BlockSpec((t, d), lambda j: (0, 0)),
                  pl.BlockSpec((1, d), lambda j: (0, 0)),
                  pl.BlockSpec((d, tn), lambda j: (0, j))],
        out_specs=pl.BlockSpec((t, tn), lambda j: (0, j)),
        out_shape=jax.ShapeDtypeStruct((t, n), F32),
        compiler_params=_params(("arbitrary",)),
        name="rms_matmul",
    )(x, g, w)


def _conv_front_kernel(x_ref, g_ref, wa_ref, wg_ref, wq_ref, hist_ref, cw_ref, cb_ref, lg_ref, lb_ref,
                       mix_ref, qx_ref, tail_ref, xn_ref, win_a, win_b, carry_ref, conv_ref,
                       *, n_glu, blk_per_seq, tr):
    i = pl.program_id(0)
    j = pl.program_id(1)
    nseq, _, tn = win_a.shape
    rows = win_a.shape[1] - CONV_HIST
    chunk = min(rows, CONV_ROW_CHUNK)
    chunks_per_seq = rows // chunk
    assert nseq == 1 or chunks_per_seq == 1
    first_block = (i % blk_per_seq) == 0
    windows = (win_a, win_b)

    def locate(t):
        if nseq == 1:
            r0 = pl.multiple_of(t * chunk, chunk)
            return 0, r0, r0
        return t, 0, pl.multiple_of(t * rows, rows)

    def load_history(p):
        cols = slice(p * tn, (p + 1) * tn)
        for sq in range(nseq):
            windows[p % 2][sq, 0:CONV_HIST, :] = jnp.where(first_block, hist_ref[sq, :, cols],
                                                           carry_ref[sq, :, cols])

    def save_tail(p):
        cols = slice(p * tn, (p + 1) * tn)
        for sq in range(nseq):
            tail = windows[p % 2][sq, rows:rows + CONV_HIST, :]
            carry_ref[sq, :, cols] = tail
            tail_ref[sq, :, cols] = tail

    def project(p, t):
        sq, r0, row = locate(t)
        xr = xn_ref[pl.ds(row, chunk), :]
        u = _dot(xr, wa_ref[...]) * _sigmoid(_dot(xr, wg_ref[...]))
        windows[p % 2][sq, pl.ds(CONV_HIST + r0, chunk), :] = u

    def convolve(p, t):
        sq, r0, row = locate(t)
        win = windows[p % 2]
        off = CONV_HIST - CONV_STATE
        span = tr + CONV_HIST
        for c0 in range(0, tn, LANE):
            ch = slice(p * tn + c0, p * tn + c0 + LANE)
            for rr in range(0, chunk, tr):
                window = win[sq, pl.ds(r0 + rr, span), c0:c0 + LANE]
                acc = jnp.zeros((tr, LANE), F32) + cb_ref[:, ch]
                for res in range(SUBLANE):
                    shifted = pltpu.roll(window, span - res, axis=0) if res else window
                    for k in range(CONV_WIDTH):
                        if (off + k) % SUBLANE == res:
                            a0 = off + k - res
                            acc = acc + cw_ref[k:k + 1, ch] * shifted[a0:a0 + tr]
                conv_ref[pl.ds(row + rr, tr), ch] = acc

    def trips(body):
        def wrapped(t, carry):
            body(t)
            return carry
        lax.fori_loop(0, nseq * chunks_per_seq, wrapped, 0)

    @pl.when(jnp.logical_and(i == 0, j == 0))
    def _():
        carry_ref[...] = jnp.zeros(carry_ref.shape, F32)

    @pl.when(j == 0)
    def _():
        xn_ref[...] = _rms(x_ref[...], g_ref[...]).astype(BF16)
        load_history(0)
        trips(lambda t: project(0, t))

    for p in range(1, n_glu):
        @pl.when(j == p)
        def _():
            load_history(p)

            def body(t):
                convolve(p - 1, t)
                project(p, t)
            trips(body)
            save_tail(p - 1)

    @pl.when(j == n_glu)
    def _():
        def body(t):
            _, _, row = locate(t)
            convolve(n_glu - 1, t)
            qx_ref[pl.ds(row, chunk), :] = _dot(xn_ref[pl.ds(row, chunk), :], wq_ref[...]).astype(BF16)
        trips(body)
        save_tail(n_glu - 1)
        y = conv_ref[...]
        yc = y - jnp.mean(y, axis=-1, keepdims=True)
        yn = yc * lax.rsqrt(jnp.mean(yc * yc, axis=-1, keepdims=True) + EPS)
        yn = yn * lg_ref[...] + lb_ref[...]
        mix_ref[...] = (yn * _sigmoid(yn)).astype(BF16)


def conv_front(x, g, w, hist, conv_w, conv_b, ln_g, ln_b, tm, tn):
    t, d = x.shape
    nseq_total, _, c = hist.shape
    s = t // nseq_total
    n_glu = c // tn
    last = n_glu - 1
    rows = min(tm, s)
    seq_per_blk = tm // rows
    blk_per_seq = s // rows
    per_seq = pl.BlockSpec((seq_per_blk, CONV_HIST, c), lambda i, j: (i // blk_per_seq, 0, 0))
    const = lambda shape: pl.BlockSpec(shape, lambda i, j: (0, 0))
    return pl.pallas_call(
        functools.partial(_conv_front_kernel, n_glu=n_glu, blk_per_seq=blk_per_seq, tr=min(rows, 64)),
        grid=(t // tm, n_glu + 1),
        in_specs=[pl.BlockSpec((tm, d), lambda i, j: (i, 0)),
                  const((1, d)),
                  pl.BlockSpec((d, tn), lambda i, j: (0, jnp.minimum(j, last))),
                  pl.BlockSpec((d, tn), lambda i, j: (0, n_glu + jnp.minimum(j, last))),
                  pl.BlockSpec((d, XATTN_WIDTH), lambda i, j: (0, 2 * c // XATTN_WIDTH)),
                  per_seq,
                  const((CONV_WIDTH, c)), const((1, c)), const((1, c)), const((1, c))],
        out_specs=[pl.BlockSpec((tm, c), lambda i, j: (i, 0)),
                   pl.BlockSpec((tm, XATTN_WIDTH), lambda i, j: (i, 0)),
                   per_seq],
        out_shape=[jax.ShapeDtypeStruct((t, c), BF16),
                   jax.ShapeDtypeStruct((t, XATTN_WIDTH), BF16),
                   jax.ShapeDtypeStruct((nseq_total, CONV_HIST, c), F32)],
        scratch_shapes=[pltpu.VMEM((tm, d), BF16),
                        pltpu.VMEM((seq_per_blk, CONV_HIST + rows, tn), F32),
                        pltpu.VMEM((seq_per_blk, CONV_HIST + rows, tn), F32),
                        pltpu.VMEM((seq_per_blk, CONV_HIST, c), F32),
                        pltpu.VMEM((tm, c), F32)],
        compiler_params=_params(("arbitrary", "arbitrary")),
        name="conv_front",
    )(x, g, w, w, w, hist, conv_w, conv_b, ln_g, ln_b)


def _fox_inproj_kernel(x_ref, g_ref, w_ref, wf_ref, bf_ref,
                       q_ref, kf_ref, kb_ref, vf_ref, vb_ref, qx_ref, lf_ref, xn_ref,
                       *, n_part, transposed, q_scale):
    j = pl.program_id(1)

    def oriented(a):
        return a.T if transposed else a

    @pl.when(j == 0)
    def _():
        xn = _rms(x_ref[...], g_ref[...]).astype(BF16)
        xn_ref[...] = xn
        f = _dot_nt(wf_ref[...], xn) + bf_ref[...]
        lf_ref[...] = jnp.minimum(f, 0.0) - jnp.log1p(jnp.exp(-jnp.abs(f)))

    z = _dot(xn_ref[...], w_ref[...])

    @pl.when(j < n_part)
    def _():
        q_ref[...] = oriented(z * q_scale).astype(BF16)

    def store_heads(ref):
        nseq, nheads, rows, _ = ref.shape
        for sq in range(nseq):
            for hh in range(nheads):
                ref[sq, hh] = z[sq * rows:(sq + 1) * rows, hh * CHUNK_HEAD:(hh + 1) * CHUNK_HEAD]

    @pl.when(jnp.logical_and(j >= n_part, j < 2 * n_part))
    def _():
        store_heads(kf_ref)
        kb_ref[...] = z.astype(BF16)

    @pl.when(jnp.logical_and(j >= 2 * n_part, j < 3 * n_part))
    def _():
        store_heads(vf_ref)
        vb_ref[...] = oriented(z).astype(BF16)

    @pl.when(j == 3 * n_part)
    def _():
        qx_ref[...] = z.astype(BF16)


def fox_inproj(x, g, w, wf_t, bf, nseq_total, tm, tn, transposed, q_scale):
    t, d = x.shape
    c = MIXER_WIDTH
    s = t // nseq_total
    n_part = c // tn
    assert tn == XATTN_WIDTH
    rows = min(tm, s)
    seq_per_blk = tm // rows
    blk_per_seq = s // rows

    def part(p):
        return lambda i, j: (i, jnp.clip(j - p * n_part, 0, n_part - 1))

    def part_t(p):
        return lambda i, j: (jnp.clip(j - p * n_part, 0, n_part - 1), i)

    def head_major(p):
        return pl.BlockSpec(
            (seq_per_blk, tn // CHUNK_HEAD, rows, CHUNK_HEAD),
            lambda i, j: (i // blk_per_seq, jnp.clip(j - p * n_part, 0, n_part - 1), i % blk_per_seq, 0))

    if transposed:
        qv_shape = jax.ShapeDtypeStruct((c, t), BF16)
        q_spec, v_spec = pl.BlockSpec((tn, tm), part_t(0)), pl.BlockSpec((tn, tm), part_t(2))
    else:
        qv_shape = jax.ShapeDtypeStruct((t, c), BF16)
        q_spec, v_spec = pl.BlockSpec((tm, tn), part(0)), pl.BlockSpec((tm, tn), part(2))
    kv_shape = jax.ShapeDtypeStruct((nseq_total, FOX_HEADS, s, CHUNK_HEAD), F32)

    return pl.pallas_call(
        functools.partial(_fox_inproj_kernel, n_part=n_part, transposed=transposed, q_scale=q_scale),
        grid=(t // tm, 3 * n_part + 1),
        in_specs=[pl.BlockSpec((tm, d), lambda i, j: (i, 0)),
                  pl.BlockSpec((1, d), lambda i, j: (0, 0)),
                  pl.BlockSpec((d, tn), lambda i, j: (0, j)),
                  pl.BlockSpec((LOGF_ROWS, d), lambda i, j: (0, 0)),
                  pl.BlockSpec((LOGF_ROWS, 1), lambda i, j: (0, 0))],
        out_specs=[q_spec,
                   head_major(1),
                   pl.BlockSpec((tm, tn), part(1)),
                   head_major(2),
                   v_spec,
                   pl.BlockSpec((tm, XATTN_WIDTH), lambda i, j: (i, 0)),
                   pl.BlockSpec((LOGF_ROWS, tm), lambda i, j: (0, i))],
        out_shape=[qv_shape,
                   kv_shape,
                   jax.ShapeDtypeStruct((t, c), BF16),
                   kv_shape,
                   qv_shape,
                   jax.ShapeDtypeStruct((t, XATTN_WIDTH), BF16),
                   jax.ShapeDtypeStruct((LOGF_ROWS, t), F32)],
        scratch_shapes=[pltpu.VMEM((tm, d), BF16)],
        compiler_params=_params(("arbitrary", "arbitrary")),
        name="fox_inproj",
    )(x, g, w, wf_t, bf)


def _neg_cumsum_kernel(x_ref, o_ref):
    x = x_ref[...]
    n = x.shape[-1]
    lane = lax.broadcasted_iota(jnp.int32, x.shape, x.ndim - 1)
    d = 1
    while d < n:
        x = x + jnp.where(lane >= d, pltpu.roll(x, d, axis=x.ndim - 1), 0.0)
        d *= 2
    o_ref[...] = -x


def neg_cumsum(x, nseq):
    r, total = x.shape
    seg = total // nseq
    return pl.pallas_call(
        _neg_cumsum_kernel,
        grid=(nseq,),
        in_specs=[pl.BlockSpec((r, seg), lambda b: (0, b))],
        out_specs=pl.BlockSpec((r, seg), lambda b: (0, b)),
        out_shape=jax.ShapeDtypeStruct((r, total), F32),
        compiler_params=_params(("arbitrary",)),
        name="neg_cumsum",
    )(x)


def _key_bias_kernel(nc_ref, o_ref):
    x = nc_ref[...] * LOG2E
    ts = x.shape[1]
    xt = jnp.concatenate([x, jnp.zeros((LANE - LOGF_ROWS, ts), F32)], axis=0).T
    lane = lax.broadcasted_iota(jnp.int32, (ts, LANE), 1)
    for h in range(FOX_HEADS):
        col = xt[:, h:h + 1]
        hi = col.astype(BF16).astype(F32)
        rest = col - hi
        mid = rest.astype(BF16).astype(F32)
        lo = rest - mid
        tile = jnp.where(lane == 0, hi, jnp.where(lane == 1, mid, jnp.where(lane == 2, lo, 0.0)))
        o_ref[0, :, h * CHUNK_HEAD:(h + 1) * CHUNK_HEAD] = tile.astype(BF16)


def key_bias(nc, b, s, ts):
    return pl.pallas_call(
        _key_bias_kernel,
        grid=(b, s // ts),
        in_specs=[pl.BlockSpec((LOGF_ROWS, ts), lambda bi, i: (0, bi * (s // ts) + i))],
        out_specs=pl.BlockSpec((1, ts, MIXER_WIDTH), lambda bi, i: (bi, i, 0)),
        out_shape=jax.ShapeDtypeStruct((b, s, MIXER_WIDTH), BF16),
        compiler_params=_params(("arbitrary", "arbitrary")),
        name="key_bias",
    )(nc)


def _fox_attn_kernel(qt_ref, k_ref, kb_ref, vt_ref, o_ref, qa_ref, sa_ref, sb_ref, m_ref, l_ref, acc_ref,
                     *, blk, hp):
    i = pl.program_id(2)
    heads = [slice(hh * CHUNK_HEAD, (hh + 1) * CHUNK_HEAD) for hh in range(hp)]

    ones = (lax.broadcasted_iota(jnp.int32, (CHUNK_HEAD, blk), 0) < BIAS_PARTS).astype(BF16)
    for hh, sl in enumerate(heads):
        qa_ref[hh, 0:CHUNK_HEAD, :] = qt_ref[sl, :]
        qa_ref[hh, CHUNK_HEAD:2 * CHUNK_HEAD, :] = ones
    m_ref[...] = jnp.full(m_ref.shape, -jnp.inf, F32)
    l_ref[...] = jnp.zeros(l_ref.shape, F32)
    acc_ref[...] = jnp.zeros(acc_ref.shape, F32)

    def scores(j, dst):
        start = pl.multiple_of(j * blk, blk)
        for hh, sl in enumerate(heads):
            ka = jnp.concatenate([k_ref[0, pl.ds(start, blk), sl], kb_ref[0, pl.ds(start, blk), sl]], axis=1)
            dst[hh] = _dot(ka, qa_ref[hh])

    def absorb(j, src, diagonal):
        start = pl.multiple_of(j * blk, blk)
        for hh, sl in enumerate(heads):
            s = src[hh]
            if diagonal:
                key = lax.broadcasted_iota(jnp.int32, s.shape, 0)
                qry = lax.broadcasted_iota(jnp.int32, s.shape, 1)
                s = jnp.where(key <= qry, s, -jnp.inf)
            m_prev = m_ref[hh]
            m_new = jnp.maximum(m_prev, jnp.max(s, axis=0, keepdims=True))
            alpha = jnp.exp2(m_prev - m_new)
            p = jnp.exp2(s - m_new)
            l_ref[hh] = alpha * l_ref[hh] + jnp.sum(p, axis=0, keepdims=True)
            acc_ref[hh] = alpha * acc_ref[hh] + _dot(vt_ref[sl, pl.ds(start, blk)], p.astype(BF16))
            m_ref[hh] = m_new

    scores(0, sa_ref)

    def pair(p, carry):
        j = 2 * p
        scores(j + 1, sb_ref)
        absorb(j, sa_ref, False)
        scores(j + 2, sa_ref)
        absorb(j + 1, sb_ref, False)
        return carry

    lax.fori_loop(0, jnp.right_shift(i, 1), pair, 0)

    @pl.when(jnp.bitwise_and(i, 1) == 0)
    def _():
        absorb(i, sa_ref, True)

    @pl.when(jnp.bitwise_and(i, 1) == 1)
    def _():
        scores(i, sb_ref)
        absorb(i - 1, sa_ref, False)
        absorb(i, sb_ref, True)

    for hh in range(hp):
        out_t = acc_ref[hh] * (1.0 / l_ref[hh])
        o_ref[0, :, hh * CHUNK_HEAD:(hh + 1) * CHUNK_HEAD] = out_t.T.astype(BF16)


def fox_attention(qt, k, kb, vt, blk, hp):
    b, s, w = k.shape
    h = w // CHUNK_HEAD
    wide = hp * CHUNK_HEAD
    nq = s // blk
    return pl.pallas_call(
        functools.partial(_fox_attn_kernel, blk=blk, hp=hp),
        grid=(b, h // hp, nq),
        in_specs=[pl.BlockSpec((wide, blk), lambda bi, hi, i: (hi, bi * nq + i)),
                  pl.BlockSpec((1, s, wide), lambda bi, hi, i: (bi, 0, hi)),
                  pl.BlockSpec((1, s, wide), lambda bi, hi, i: (bi, 0, hi)),
                  pl.BlockSpec((wide, s), lambda bi, hi, i: (hi, bi))],
        out_specs=pl.BlockSpec((1, blk, wide), lambda bi, hi, i: (bi, i, hi)),
        out_shape=jax.ShapeDtypeStruct((b, s, w), BF16),
        scratch_shapes=[pltpu.VMEM((hp, 2 * CHUNK_HEAD, blk), BF16),
                        pltpu.VMEM((hp, blk, blk), F32), pltpu.VMEM((hp, blk, blk), F32),
                        pltpu.VMEM((hp, 1, blk), F32), pltpu.VMEM((hp, 1, blk), F32),
                        pltpu.VMEM((hp, CHUNK_HEAD, blk), F32)],
        compiler_params=_params(("arbitrary", "arbitrary", "arbitrary")),
        name="fox_attention",
    )(qt, k, kb, vt)


def _fox_decode_kernel(q_ref, kc_ref, vc_ref, kn_ref, vn_ref, nc_ref, o_ref, *, past, t, heads):
    row = lax.broadcasted_iota(jnp.int32, (t, t), 0)
    col = lax.broadcasted_iota(jnp.int32, (t, t), 1)
    for h in range(heads):
        sl = slice(h * CHUNK_HEAD, (h + 1) * CHUNK_HEAD)
        q = q_ref[0, :, sl]
        s1 = _dot_nt(q, kc_ref[0, h].astype(BF16)) + nc_ref[h, :, 0:past]
        s2 = _dot_nt(q, kn_ref[0, :, sl]) + nc_ref[h, :, past:past + t]
        s2 = jnp.where(col <= row, s2, -jnp.inf)
        m = jnp.maximum(jnp.max(s1, axis=-1, keepdims=True), jnp.max(s2, axis=-1, keepdims=True))
        p1 = jnp.exp(s1 - m)
        p2 = jnp.exp(s2 - m)
        l = jnp.sum(p1, axis=-1, keepdims=True) + jnp.sum(p2, axis=-1, keepdims=True)
        o = _dot(p1.astype(BF16), vc_ref[0, h].astype(BF16)) + _dot(p2.astype(BF16), vn_ref[0, :, sl])
        o_ref[0, :, sl] = (o / l).astype(BF16)


def fox_decode_attention(q, kc, vc, kn, vn, nc):
    b, t, w = q.shape
    _, h, past, dh = kc.shape
    npad = nc.shape[-1]
    new = pl.BlockSpec((1, t, w), lambda bi: (bi, 0, 0))
    old = pl.BlockSpec((1, h, past, dh), lambda bi: (bi, 0, 0, 0))
    return pl.pallas_call(
        functools.partial(_fox_decode_kernel, past=past, t=t, heads=h),
        grid=(b,),
        in_specs=[new, old, old, new, new,
                  pl.BlockSpec((LOGF_ROWS, 1, npad), lambda bi: (bi, 0, 0))],
        out_specs=new,
        out_shape=jax.ShapeDtypeStruct((b, t, w), BF16),
        compiler_params=_params(("arbitrary",)),
        name="fox_decode_attention",
    )(q, kc, vc, kn, vn, nc)


def _mem_attn_kernel(q_ref, mk_ref, mv_ref, o_ref):
    for h in range(XATTN_HEADS):
        sl = slice(h * CHUNK_HEAD, (h + 1) * CHUNK_HEAD)
        s = _dot_nt(q_ref[0, :, sl], mk_ref[0, :, sl]) * ATTN_SCALE
        p = jnp.exp(s - jnp.max(s, axis=-1, keepdims=True))
        l = jnp.sum(p, axis=-1, keepdims=True)
        o = _dot(p.astype(BF16), mv_ref[0, :, sl])
        o_ref[0, :, sl] = (o / l).astype(BF16)


def mem_attention(qx, mk, mv, tm):
    b, t, w = qx.shape
    nm = mk.shape[1]
    return pl.pallas_call(
        _mem_attn_kernel,
        grid=(b, t // tm),
        in_specs=[pl.BlockSpec((1, tm, w), lambda bi, i: (bi, i, 0)),
                  pl.BlockSpec((1, nm, w), lambda bi, i: (bi, 0, 0)),
                  pl.BlockSpec((1, nm, w), lambda bi, i: (bi, 0, 0))],
        out_specs=pl.BlockSpec((1, tm, w), lambda bi, i: (bi, i, 0)),
        out_shape=jax.ShapeDtypeStruct((b, t, w), BF16),
        compiler_params=_params(("arbitrary", "arbitrary")),
        name="mem_attention",
    )(qx, mk, mv)


def _out_proj_kernel(x_ref, mix_ref, xo_ref, w1_ref, w2_ref, o_ref):
    o_ref[...] = x_ref[...] + _dot(mix_ref[...], w1_ref[...]) + _dot(xo_ref[...], w2_ref[...])


def out_proj(x, mix, xo, w, layer, tm):
    t, d = x.shape
    c = mix.shape[1]
    return pl.pallas_call(
        _out_proj_kernel,
        grid=(t // tm,),
        in_specs=[pl.BlockSpec((tm, d), lambda i: (i, 0)),
                  pl.BlockSpec((tm, c), lambda i: (i, 0)),
                  pl.BlockSpec((tm, XATTN_WIDTH), lambda i: (i, 0)),
                  pl.BlockSpec((None, c, d), lambda i: (layer, 0, 0)),
                  pl.BlockSpec((None, XATTN_WIDTH, d), lambda i: (layer, c // XATTN_WIDTH, 0))],
        out_specs=pl.BlockSpec((tm, d), lambda i: (i, 0)),
        out_shape=jax.ShapeDtypeStruct((t, d), F32),
        compiler_params=_params(("arbitrary",)),
        name="out_proj",
    )(x, mix, xo, w, w)


def _mlp_kernel(x_ref, g_ref, wu_ref, wd_ref, gf_ref, o_ref, xn_ref, *, final_norm):
    f = pl.program_id(1)

    @pl.when(f == 0)
    def _():
        x = x_ref[...]
        xn_ref[...] = _rms(x, g_ref[...]).astype(BF16)
        o_ref[...] = x

    h = jnp.maximum(_dot(xn_ref[...], wu_ref[...]), 0.0)
    o_ref[...] += _dot((h * h).astype(BF16), wd_ref[...])

    if final_norm:
        @pl.when(f == pl.num_programs(1) - 1)
        def _():
            o_ref[...] = _rms(o_ref[...], gf_ref[...])


def mlp(x, g, w_up, w_down, g_final, layer, tm, tf, final_norm):
    t, d = x.shape
    ff = w_up.shape[2]
    return pl.pallas_call(
        functools.partial(_mlp_kernel, final_norm=final_norm),
        grid=(t // tm, ff // tf),
        in_specs=[pl.BlockSpec((tm, d), lambda i, f: (i, 0)),
                  pl.BlockSpec((1, d), lambda i, f: (0, 0)),
                  pl.BlockSpec((None, d, tf), lambda i, f: (layer, 0, f)),
                  pl.BlockSpec((None, tf, d), lambda i, f: (layer, f, 0)),
                  pl.BlockSpec((1, d), lambda i, f: (0, 0))],
        out_specs=pl.BlockSpec((tm, d), lambda i, f: (i, 0)),
        out_shape=jax.ShapeDtypeStruct((t, d), F32),
        scratch_shapes=[pltpu.VMEM((tm, d), BF16)],
        compiler_params=_params(("arbitrary", "arbitrary")),
        name="mlp",
    )(x, g, w_up, w_down, g_final)


def _token_tile(t, pref):
    return pref if t % pref == 0 else t


def kernel(x_prompt, x_sample, cache_mem_k, cache_mem_v, state_conv, cache_fox_k, cache_fox_v, cache_fox_logf, mem_prompt, g_mix, g_mem, w_mem_k, w_mem_v, w_in_conv, conv_w, conv_b, conv_ln_g, conv_ln_b, w_in_fox, b_fox_f, w_out, g_mlp, w_up, w_down, g_final):
    bp, sp, d = x_prompt.shape
    bs, ss, _ = x_sample.shape
    depth = g_mix.shape[0]
    n_mem = mem_prompt.shape[1]
    past = cache_fox_k.shape[2]
    c = MIXER_WIDTH

    groups = {
        "p": dict(y=x_prompt.reshape(bp * sp, d), b=bp, s=sp, tm_conv=512, tm_fox=1024),
        "s": dict(y=x_sample.reshape(bs * ss, d), b=bs, s=ss, tm_conv=bs * ss, tm_fox=bs * ss),
    }
    mem_flat = mem_prompt.reshape(bp * n_mem, d)
    row = lambda a: a.reshape(1, -1)
    w_o, w_u, w_d = w_out.astype(BF16), w_up.astype(BF16), w_down.astype(BF16)

    mem_k_p, mem_v_p, conv_p, conv_s = [], [], [], []
    fk, fv, fl = {"p": [], "s": []}, {"p": [], "s": []}, {"p": [], "s": []}

    for i in range(depth):
        j = i // 2
        w_kv = jnp.concatenate([w_mem_k[i], w_mem_v[i]], axis=1).astype(BF16)
        kv = rms_matmul(mem_flat, row(g_mem[i]), w_kv, XATTN_WIDTH)
        mk_p = kv[:, :XATTN_WIDTH].reshape(bp, n_mem, XATTN_WIDTH)
        mv_p = kv[:, XATTN_WIDTH:].reshape(bp, n_mem, XATTN_WIDTH)
        mem_k_p.append(mk_p.reshape(bp, n_mem, XATTN_HEADS, CHUNK_HEAD))
        mem_v_p.append(mv_p.reshape(bp, n_mem, XATTN_HEADS, CHUNK_HEAD))
        mem = {"p": (mk_p.astype(BF16), mv_p.astype(BF16)),
               "s": (cache_mem_k[i].reshape(bs, n_mem, XATTN_WIDTH).astype(BF16),
                     cache_mem_v[i].reshape(bs, n_mem, XATTN_WIDTH).astype(BF16))}
        if i % 2 == 0:
            w_in = w_in_conv[j].astype(BF16)
        else:
            wf = w_in_fox[j]
            w_in = jnp.concatenate([wf[:, :3 * c].astype(BF16), wf[:, 3 * c + FOX_HEADS:].astype(BF16)], axis=1)
            wf_t = jnp.pad(wf[:, 3 * c:3 * c + FOX_HEADS].astype(BF16).T, ((0, LOGF_ROWS - FOX_HEADS), (0, 0)))
            bf = jnp.zeros((LOGF_ROWS, 1), F32).at[:FOX_HEADS, 0].set(b_fox_f[j])

        for name, grp in groups.items():
            y, b, s = grp["y"], grp["b"], grp["s"]
            if i % 2 == 0:
                if name == "p":
                    hist = jnp.zeros((b, CONV_HIST, c), F32)
                else:
                    hist = jnp.concatenate([jnp.zeros((b, CONV_HIST - CONV_STATE, c), F32), state_conv[j]], axis=1)
                mix, qx, tail = conv_front(y, row(g_mix[i]), w_in, hist, conv_w[j], row(conv_b[j]),
                                           row(conv_ln_g[j]), row(conv_ln_b[j]), grp["tm_conv"], XATTN_WIDTH)
                (conv_p if name == "p" else conv_s).append(tail[:, CONV_HIST - CONV_STATE:])
            else:
                q, k_f, k_b, v_f, v_b, qx, lf_t = fox_inproj(
                    y, row(g_mix[i]), w_in, wf_t, bf, b, grp["tm_fox"], XATTN_WIDTH, transposed=(name == "p"),
                    q_scale=ATTN_SCALE * LOG2E if name == "p" else ATTN_SCALE)
                fk[name].append(k_f.transpose(0, 2, 1, 3))
                fv[name].append(v_f.transpose(0, 2, 1, 3))
                fl[name].append(lf_t[:FOX_HEADS].reshape(FOX_HEADS, b, s).transpose(1, 2, 0))
                k_b = k_b.reshape(b, s, c)
                if name == "p":
                    nc = neg_cumsum(lf_t, b)
                    mix = fox_attention(q, k_b, key_bias(nc, b, s, 512), v_b, 512, 3)
                else:
                    q = q.reshape(b, s, c)
                    v_b = v_b.reshape(b, s, c)
                    total = past + s
                    padded = -(-total // LANE) * LANE
                    lf_old = jnp.zeros((b, LOGF_ROWS, past), F32).at[:, :FOX_HEADS].set(
                        cache_fox_logf[j].astype(F32).transpose(0, 2, 1))
                    lf_all = jnp.concatenate(
                        [lf_old, lf_t.reshape(LOGF_ROWS, b, s).transpose(1, 0, 2),
                         jnp.zeros((b, LOGF_ROWS, padded - total), F32)], axis=2)
                    lf_all = lf_all.transpose(1, 0, 2).reshape(LOGF_ROWS, b * padded)
                    nc = neg_cumsum(lf_all, b)
                    nc = nc.reshape(LOGF_ROWS, b, padded).transpose(1, 0, 2).reshape(b * LOGF_ROWS, 1, padded)
                    mix = fox_decode_attention(q, cache_fox_k[j].transpose(0, 2, 1, 3),
                                               cache_fox_v[j].transpose(0, 2, 1, 3), k_b, v_b, nc)
            qx = qx.reshape(b, s, XATTN_WIDTH)
            xo = mem_attention(qx, mem[name][0], mem[name][1], _token_tile(s, 1024))
            y = out_proj(y, mix.reshape(b * s, c), xo.reshape(b * s, XATTN_WIDTH), w_o, i, _token_tile(b * s, 512))
            y = mlp(y, row(g_mlp[i]), w_u, w_d, row(g_final), i, _token_tile(b * s, 512), 1024,
                    final_norm=(i == depth - 1))
            grp["y"] = y

    y_prompt = groups["p"]["y"].reshape(bp, sp, d)
    y_sample = groups["s"]["y"].reshape(bs, ss, d)
    return (y_prompt, y_sample, jnp.stack(mem_k_p), jnp.stack(mem_v_p), jnp.stack(conv_p), jnp.stack(conv_s),
            jnp.stack(fk["p"]), jnp.stack(fv["p"]), jnp.stack(fl["p"]),
            jnp.stack(fk["s"]), jnp.stack(fv["s"]), jnp.stack(fl["s"]))
```

```python
import functools

import jax
import jax.numpy as jnp
from jax import lax
from jax.experimental import pallas as pl
from jax.experimental.pallas import tpu as pltpu

F32 = jnp.float32
BF16 = jnp.bfloat16

D_MODEL = 2048
CHUNK_HEAD = 128
XATTN_HEADS = 4
XATTN_WIDTH = XATTN_HEADS * CHUNK_HEAD
MIXER_WIDTH = D_MODEL - XATTN_WIDTH
FOX_HEADS = MIXER_WIDTH // CHUNK_HEAD
CONV_WIDTH = 31
CONV_STATE = CONV_WIDTH - 1
CONV_HIST = 32
D_FF = 4 * D_MODEL
EPS = 1e-6
ATTN_SCALE = CHUNK_HEAD ** -0.5
LOG2E = 1.4426950408889634
LOGF_ROWS = 16
LANE = 128
SUBLANE = 8
BIAS_PARTS = 3

VMEM_LIMIT = 56 * 1024 * 1024


def _params(sem):
    return pltpu.CompilerParams(dimension_semantics=sem, vmem_limit_bytes=VMEM_LIMIT)


def _rms(x, g):
    ms = jnp.mean(x * x, axis=-1, keepdims=True)
    return x * lax.rsqrt(ms + EPS) * g


def _sigmoid(x):
    return 1.0 / (1.0 + jnp.exp(-x))


def _dot(a, b):
    return jnp.dot(a, b, preferred_element_type=F32)


def _dot_nt(a, b):
    return lax.dot_general(a, b, (((1,), (1,)), ((), ())), preferred_element_type=F32)


def _rms_matmul_kernel(x_ref, g_ref, w_ref, o_ref):
    xn = _rms(x_ref[...], g_ref[...]).astype(BF16)
    o_ref[...] = _dot(xn, w_ref[...])


def rms_matmul(x, g, w, tn):
    t, d = x.shape
    n = w.shape[1]
    return pl.pallas_call(
        _rms_matmul_kernel,
        grid=(n // tn,),
        in_specs=[pl.BlockSpec((t, d), lambda j: (0, 0)),
                  pl.BlockSpec((1, d), lambda j: (0, 0)),
                  pl.BlockSpec((d, tn), lambda j: (0, j))],
        out_specs=pl.BlockSpec((t, tn), lambda j: (0, j)),
        out_shape=jax.ShapeDtypeStruct((t, n), F32),
        compiler_params=_params(("arbitrary",)),
        name="rms_matmul",
    )(x, g, w)


def _conv_front_kernel(x_ref, g_ref, wa_ref, wg_ref, wq_ref, hist_ref, cw_ref, cb_ref, lg_ref, lb_ref,
                       mix_ref, qx_ref, tail_ref, xn_ref, win_a, win_b, carry_ref, conv_ref,
                       *, n_glu, blk_per_seq, tr):
    i = pl.program_id(0)
    j = pl.program_id(1)
    nseq, _, tn = win_a.shape
    rows = win_a.shape[1] - CONV_HIST
    first_block = (i % blk_per_seq) == 0
    windows = (win_a, win_b)

    def project(p):
        xn = xn_ref[...]
        u = _dot(xn, wa_ref[...]) * _sigmoid(_dot(xn, wg_ref[...]))
        win = windows[p % 2]
        cols = slice(p * tn, (p + 1) * tn)
        for sq in range(nseq):
            win[sq, 0:CONV_HIST, :] = jnp.where(first_block, hist_ref[sq, :, cols], carry_ref[sq, :, cols])
            win[sq, CONV_HIST:CONV_HIST + rows, :] = u[sq * rows:(sq + 1) * rows]

    def convolve(p):
        win = windows[p % 2]
        off = CONV_HIST - CONV_STATE
        span = tr + CONV_HIST
        for sq in range(nseq):
            for c0 in range(0, tn, LANE):
                ch = slice(p * tn + c0, p * tn + c0 + LANE)
                for r0 in range(0, rows, tr):
                    window = win[sq, r0:r0 + span, c0:c0 + LANE]
                    acc = jnp.zeros((tr, LANE), F32) + cb_ref[:, ch]
                    for res in range(SUBLANE):
                        shifted = pltpu.roll(window, span - res, axis=0) if res else window
                        for k in range(CONV_WIDTH):
                            if (off + k) % SUBLANE == res:
                                a0 = off + k - res
                                acc = acc + cw_ref[k:k + 1, ch] * shifted[a0:a0 + tr]
                    conv_ref[sq * rows + r0:sq * rows + r0 + tr, ch] = acc
            tail = win[sq, rows:rows + CONV_HIST, :]
            carry_ref[sq, :, p * tn:(p + 1) * tn] = tail
            tail_ref[sq, :, p * tn:(p + 1) * tn] = tail

    @pl.when(jnp.logical_and(i == 0, j == 0))
    def _():
        carry_ref[...] = jnp.zeros(carry_ref.shape, F32)

    @pl.when(j == 0)
    def _():
        xn_ref[...] = _rms(x_ref[...], g_ref[...]).astype(BF16)
        project(0)

    for p in range(1, n_glu):
        @pl.when(j == p)
        def _():
            convolve(p - 1)
            project(p)

    @pl.when(j == n_glu)
    def _():
        convolve(n_glu - 1)
        qx_ref[...] = _dot(xn_ref[...], wq_ref[...]).astype(BF16)
        y = conv_ref[...]
        yc = y - jnp.mean(y, axis=-1, keepdims=True)
        yn = yc * lax.rsqrt(jnp.mean(yc * yc, axis=-1, keepdims=True) + EPS)
        yn = yn * lg_ref[...] + lb_ref[...]
        mix_ref[...] = (yn * _sigmoid(yn)).astype(BF16)


def conv_front(x, g, w, hist, conv_w, conv_b, ln_g, ln_b, tm, tn):
    t, d = x.shape
    nseq_total, _, c = hist.shape
    s = t // nseq_total
    n_glu = c // tn
    last = n_glu - 1
    rows = min(tm, s)
    seq_per_blk = tm // rows
    blk_per_seq = s // rows
    per_seq = pl.BlockSpec((seq_per_blk, CONV_HIST, c), lambda i, j: (i // blk_per_seq, 0, 0))
    const = lambda shape: pl.BlockSpec(shape, lambda i, j: (0, 0))
    return pl.pallas_call(
        functools.partial(_conv_front_kernel, n_glu=n_glu, blk_per_seq=blk_per_seq, tr=min(rows, 64)),
        grid=(t // tm, n_glu + 1),
        in_specs=[pl.BlockSpec((tm, d), lambda i, j: (i, 0)),
                  const((1, d)),
                  pl.BlockSpec((d, tn), lambda i, j: (0, jnp.minimum(j, last))),
                  pl.BlockSpec((d, tn), lambda i, j: (0, n_glu + jnp.minimum(j, last))),
                  pl.BlockSpec((d, XATTN_WIDTH), lambda i, j: (0, 2 * c // XATTN_WIDTH)),
                  per_seq,
                  const((CONV_WIDTH, c)), const((1, c)), const((1, c)), const((1, c))],
        out_specs=[pl.BlockSpec((tm, c), lambda i, j: (i, 0)),
                   pl.BlockSpec((tm, XATTN_WIDTH), lambda i, j: (i, 0)),
                   per_seq],
        out_shape=[jax.ShapeDtypeStruct((t, c), BF16),
                   jax.ShapeDtypeStruct((t, XATTN_WIDTH), BF16),
                   jax.ShapeDtypeStruct((nseq_total, CONV_HIST, c), F32)],
        scratch_shapes=[pltpu.VMEM((tm, d), BF16),
                        pltpu.VMEM((seq_per_blk, CONV_HIST + rows, tn), F32),
                        pltpu.VMEM((seq_per_blk, CONV_HIST + rows, tn), F32),
                        pltpu.VMEM((seq_per_blk, CONV_HIST, c), F32),
                        pltpu.VMEM((tm, c), F32)],
        compiler_params=_params(("arbitrary", "arbitrary")),
        name="conv_front",
    )(x, g, w, w, w, hist, conv_w, conv_b, ln_g, ln_b)


def _fox_inproj_kernel(x_ref, g_ref, w_ref, wf_ref, bf_ref,
                       q_ref, kf_ref, kb_ref, vf_ref, vb_ref, qx_ref, lf_ref, xn_ref,
                       *, n_part, transposed, q_scale):
    j = pl.program_id(1)

    def oriented(a):
        return a.T if transposed else a

    @pl.when(j == 0)
    def _():
        xn = _rms(x_ref[...], g_ref[...]).astype(BF16)
        xn_ref[...] = xn
        f = _dot_nt(wf_ref[...], xn) + bf_ref[...]
        lf_ref[...] = jnp.minimum(f, 0.0) - jnp.log1p(jnp.exp(-jnp.abs(f)))

    z = _dot(xn_ref[...], w_ref[...])

    @pl.when(j < n_part)
    def _():
        q_ref[...] = oriented(z * q_scale).astype(BF16)

    def store_heads(ref):
        nseq, nheads, rows, _ = ref.shape
        for sq in range(nseq):
            for hh in range(nheads):
                ref[sq, hh] = z[sq * rows:(sq + 1) * rows, hh * CHUNK_HEAD:(hh + 1) * CHUNK_HEAD]

    @pl.when(jnp.logical_and(j >= n_part, j < 2 * n_part))
    def _():
        store_heads(kf_ref)
        kb_ref[...] = z.astype(BF16)

    @pl.when(jnp.logical_and(j >= 2 * n_part, j < 3 * n_part))
    def _():
        store_heads(vf_ref)
        vb_ref[...] = oriented(z).astype(BF16)

    @pl.when(j == 3 * n_part)
    def _():
        qx_ref[...] = z.astype(BF16)


def fox_inproj(x, g, w, wf_t, bf, nseq_total, tm, tn, transposed, q_scale):
    t, d = x.shape
    c = MIXER_WIDTH
    s = t // nseq_total
    n_part = c // tn
    assert tn == XATTN_WIDTH
    rows = min(tm, s)
    seq_per_blk = tm // rows
    blk_per_seq = s // rows

    def part(p):
        return lambda i, j: (i, jnp.clip(j - p * n_part, 0, n_part - 1))

    def part_t(p):
        return lambda i, j: (jnp.clip(j - p * n_part, 0, n_part - 1), i)

    def head_major(p):
        return pl.BlockSpec(
            (seq_per_blk, tn // CHUNK_HEAD, rows, CHUNK_HEAD),
            lambda i, j: (i // blk_per_seq, jnp.clip(j - p * n_part, 0, n_part - 1), i % blk_per_seq, 0))

    if transposed:
        qv_shape = jax.ShapeDtypeStruct((c, t), BF16)
        q_spec, v_spec = pl.BlockSpec((tn, tm), part_t(0)), pl.BlockSpec((tn, tm), part_t(2))
    else:
        qv_shape = jax.ShapeDtypeStruct((t, c), BF16)
        q_spec, v_spec = pl.BlockSpec((tm, tn), part(0)), pl.BlockSpec((tm, tn), part(2))
    kv_shape = jax.ShapeDtypeStruct((nseq_total, FOX_HEADS, s, CHUNK_HEAD), F32)

    return pl.pallas_call(
        functools.partial(_fox_inproj_kernel, n_part=n_part, transposed=transposed, q_scale=q_scale),
        grid=(t // tm, 3 * n_part + 1),
        in_specs=[pl.BlockSpec((tm, d), lambda i, j: (i, 0)),
                  pl.BlockSpec((1, d), lambda i, j: (0, 0)),
                  pl.BlockSpec((d, tn), lambda i, j: (0, j)),
                  pl.BlockSpec((LOGF_ROWS, d), lambda i, j: (0, 0)),
                  pl.BlockSpec((LOGF_ROWS, 1), lambda i, j: (0, 0))],
        out_specs=[q_spec,
                   head_major(1),
                   pl.BlockSpec((tm, tn), part(1)),
                   head_major(2),
                   v_spec,
                   pl.BlockSpec((tm, XATTN_WIDTH), lambda i, j: (i, 0)),
                   pl.BlockSpec((LOGF_ROWS, tm), lambda i, j: (0, i))],
        out_shape=[qv_shape,
                   kv_shape,
                   jax.ShapeDtypeStruct((t, c), BF16),
                   kv_shape,
                   qv_shape,
                   jax.ShapeDtypeStruct((t, XATTN_WIDTH), BF16),
                   jax.ShapeDtypeStruct((LOGF_ROWS, t), F32)],
        scratch_shapes=[pltpu.VMEM((tm, d), BF16)],
        compiler_params=_params(("arbitrary", "arbitrary")),
        name="fox_inproj",
    )(x, g, w, wf_t, bf)


def _neg_cumsum_kernel(x_ref, o_ref):
    x = x_ref[...]
    n = x.shape[-1]
    lane = lax.broadcasted_iota(jnp.int32, x.shape, x.ndim - 1)
    d = 1
    while d < n:
        x = x + jnp.where(lane >= d, pltpu.roll(x, d, axis=x.ndim - 1), 0.0)
        d *= 2
    o_ref[...] = -x


def neg_cumsum(x, nseq):
    r, total = x.shape
    seg = total // nseq
    return pl.pallas_call(
        _neg_cumsum_kernel,
        grid=(nseq,),
        in_specs=[pl.BlockSpec((r, seg), lambda b: (0, b))],
        out_specs=pl.BlockSpec((r, seg), lambda b: (0, b)),
        out_shape=jax.ShapeDtypeStruct((r, total), F32),
        compiler_params=_params(("arbitrary",)),
        name="neg_cumsum",
    )(x)


def _key_bias_kernel(nc_ref, place_ref, o_ref):
    x = nc_ref[...] * LOG2E
    hi = x.astype(BF16)
    rest = x - hi.astype(F32)
    mid = rest.astype(BF16)
    lo = (rest - mid.astype(F32)).astype(BF16)
    parts = jnp.concatenate([hi, mid, lo], axis=0)
    tiles = lax.dot_general(parts, place_ref[...], (((0,), (0,)), ((), ())), preferred_element_type=F32)
    o_ref[0] = tiles.astype(BF16)


def key_bias(nc, b, s, ts):
    part = jnp.arange(BIAS_PARTS * LOGF_ROWS) // LOGF_ROWS
    head = jnp.arange(BIAS_PARTS * LOGF_ROWS) % LOGF_ROWS
    lane = jnp.arange(MIXER_WIDTH)
    place = ((lane[None, :] == (head * CHUNK_HEAD + part)[:, None]) & (head < FOX_HEADS)[:, None]).astype(BF16)
    return pl.pallas_call(
        _key_bias_kernel,
        grid=(b, s // ts),
        in_specs=[pl.BlockSpec((LOGF_ROWS, ts), lambda bi, i: (0, bi * (s // ts) + i)),
                  pl.BlockSpec((BIAS_PARTS * LOGF_ROWS, MIXER_WIDTH), lambda bi, i: (0, 0))],
        out_specs=pl.BlockSpec((1, ts, MIXER_WIDTH), lambda bi, i: (bi, i, 0)),
        out_shape=jax.ShapeDtypeStruct((b, s, MIXER_WIDTH), BF16),
        compiler_params=_params(("arbitrary", "arbitrary")),
        name="key_bias",
    )(nc, place)


def _fox_attn_kernel(qt_ref, k_ref, kb_ref, vt_ref, o_ref, qa_ref, sa_ref, sb_ref, m_ref, l_ref, acc_ref,
                     *, blk, hp):
    i = pl.program_id(2)
    heads = [slice(hh * CHUNK_HEAD, (hh + 1) * CHUNK_HEAD) for hh in range(hp)]

    ones = (lax.broadcasted_iota(jnp.int32, (CHUNK_HEAD, blk), 0) < BIAS_PARTS).astype(BF16)
    for hh, sl in enumerate(heads):
        qa_ref[hh, 0:CHUNK_HEAD, :] = qt_ref[sl, :]
        qa_ref[hh, CHUNK_HEAD:2 * CHUNK_HEAD, :] = ones
    m_ref[...] = jnp.full(m_ref.shape, -jnp.inf, F32)
    l_ref[...] = jnp.zeros(l_ref.shape, F32)
    acc_ref[...] = jnp.zeros(acc_ref.shape, F32)

    def scores(j, dst):
        start = pl.multiple_of(j * blk, blk)
        for hh, sl in enumerate(heads):
            ka = jnp.concatenate([k_ref[0, pl.ds(start, blk), sl], kb_ref[0, pl.ds(start, blk), sl]], axis=1)
            dst[hh] = _dot(ka, qa_ref[hh])

    def absorb(j, src, diagonal):
        start = pl.multiple_of(j * blk, blk)
        for hh, sl in enumerate(heads):
            s = src[hh]
            if diagonal:
                key = lax.broadcasted_iota(jnp.int32, s.shape, 0)
                qry = lax.broadcasted_iota(jnp.int32, s.shape, 1)
                s = jnp.where(key <= qry, s, -jnp.inf)
            m_prev = m_ref[hh]
            m_new = jnp.maximum(m_prev, jnp.max(s, axis=0, keepdims=True))
            alpha = jnp.exp2(m_prev - m_new)
            p = jnp.exp2(s - m_new)
            l_ref[hh] = alpha * l_ref[hh] + jnp.sum(p, axis=0, keepdims=True)
            acc_ref[hh] = alpha * acc_ref[hh] + _dot(vt_ref[sl, pl.ds(start, blk)], p.astype(BF16))
            m_ref[hh] = m_new

    scores(0, sa_ref)

    def pair(p, carry):
        j = 2 * p
        scores(j + 1, sb_ref)
        absorb(j, sa_ref, False)
        scores(j + 2, sa_ref)
        absorb(j + 1, sb_ref, False)
        return carry

    lax.fori_loop(0, jnp.right_shift(i, 1), pair, 0)

    @pl.when(jnp.bitwise_and(i, 1) == 0)
    def _():
        absorb(i, sa_ref, True)

    @pl.when(jnp.bitwise_and(i, 1) == 1)
    def _():
        scores(i, sb_ref)
        absorb(i - 1, sa_ref, False)
        absorb(i, sb_ref, True)

    for hh in range(hp):
        out_t = acc_ref[hh] * (1.0 / l_ref[hh])
        o_ref[0, :, hh * CHUNK_HEAD:(hh + 1) * CHUNK_HEAD] = out_t.T.astype(BF16)


def fox_attention(qt, k, kb, vt, blk, hp):
    b, s, w = k.shape
    h = w // CHUNK_HEAD
    wide = hp * CHUNK_HEAD
    nq = s // blk
    return pl.pallas_call(
        functools.partial(_fox_attn_kernel, blk=blk, hp=hp),
        grid=(b, h // hp, nq),
        in_specs=[pl.BlockSpec((wide, blk), lambda bi, hi, i: (hi, bi * nq + i)),
                  pl.BlockSpec((1, s, wide), lambda bi, hi, i: (bi, 0, hi)),
                  pl.BlockSpec((1, s, wide), lambda bi, hi, i: (bi, 0, hi)),
                  pl.BlockSpec((wide, s), lambda bi, hi, i: (hi, bi))],
        out_specs=pl.BlockSpec((1, blk, wide), lambda bi, hi, i: (bi, i, hi)),
        out_shape=jax.ShapeDtypeStruct((b, s, w), BF16),
        scratch_shapes=[pltpu.VMEM((hp, 2 * CHUNK_HEAD, blk), BF16),
                        pltpu.VMEM((hp, blk, blk), F32), pltpu.VMEM((hp, blk, blk), F32),
                        pltpu.VMEM((hp, 1, blk), F32), pltpu.VMEM((hp, 1, blk), F32),
                        pltpu.VMEM((hp, CHUNK_HEAD, blk), F32)],
        compiler_params=_params(("arbitrary", "arbitrary", "arbitrary")),
        name="fox_attention",
    )(qt, k, kb, vt)


def _fox_decode_kernel(q_ref, kc_ref, vc_ref, kn_ref, vn_ref, nc_ref, o_ref, *, past, t, heads):
    row = lax.broadcasted_iota(jnp.int32, (t, t), 0)
    col = lax.broadcasted_iota(jnp.int32, (t, t), 1)
    for h in range(heads):
        sl = slice(h * CHUNK_HEAD, (h + 1) * CHUNK_HEAD)
        q = q_ref[0, :, sl]
        s1 = _dot_nt(q, kc_ref[0, h].astype(BF16)) + nc_ref[h, :, 0:past]
        s2 = _dot_nt(q, kn_ref[0, :, sl]) + nc_ref[h, :, past:past + t]
        s2 = jnp.where(col <= row, s2, -jnp.inf)
        m = jnp.maximum(jnp.max(s1, axis=-1, keepdims=True), jnp.max(s2, axis=-1, keepdims=True))
        p1 = jnp.exp(s1 - m)
        p2 = jnp.exp(s2 - m)
        l = jnp.sum(p1, axis=-1, keepdims=True) + jnp.sum(p2, axis=-1, keepdims=True)
        o = _dot(p1.astype(BF16), vc_ref[0, h].astype(BF16)) + _dot(p2.astype(BF16), vn_ref[0, :, sl])
        o_ref[0, :, sl] = (o / l).astype(BF16)


def fox_decode_attention(q, kc, vc, kn, vn, nc):
    b, t, w = q.shape
    _, h, past, dh = kc.shape
    npad = nc.shape[-1]
    new = pl.BlockSpec((1, t, w), lambda bi: (bi, 0, 0))
    old = pl.BlockSpec((1, h, past, dh), lambda bi: (bi, 0, 0, 0))
    return pl.pallas_call(
        functools.partial(_fox_decode_kernel, past=past, t=t, heads=h),
        grid=(b,),
        in_specs=[new, old, old, new, new,
                  pl.BlockSpec((LOGF_ROWS, 1, npad), lambda bi: (bi, 0, 0))],
        out_specs=new,
        out_shape=jax.ShapeDtypeStruct((b, t, w), BF16),
        compiler_params=_params(("arbitrary",)),
        name="fox_decode_attention",
    )(q, kc, vc, kn, vn, nc)


def _mem_attn_kernel(q_ref, mk_ref, mv_ref, o_ref):
    for h in range(XATTN_HEADS):
        sl = slice(h * CHUNK_HEAD, (h + 1) * CHUNK_HEAD)
        s = _dot_nt(q_ref[0, :, sl], mk_ref[0, :, sl]) * ATTN_SCALE
        p = jnp.exp(s - jnp.max(s, axis=-1, keepdims=True))
        l = jnp.sum(p, axis=-1, keepdims=True)
        o = _dot(p.astype(BF16), mv_ref[0, :, sl])
        o_ref[0, :, sl] = (o / l).astype(BF16)


def mem_attention(qx, mk, mv, tm):
    b, t, w = qx.shape
    nm = mk.shape[1]
    return pl.pallas_call(
        _mem_attn_kernel,
        grid=(b, t // tm),
        in_specs=[pl.BlockSpec((1, tm, w), lambda bi, i: (bi, i, 0)),
                  pl.BlockSpec((1, nm, w), lambda bi, i: (bi, 0, 0)),
                  pl.BlockSpec((1, nm, w), lambda bi, i: (bi, 0, 0))],
        out_specs=pl.BlockSpec((1, tm, w), lambda bi, i: (bi, i, 0)),
        out_shape=jax.ShapeDtypeStruct((b, t, w), BF16),
        compiler_params=_params(("arbitrary", "arbitrary")),
        name="mem_attention",
    )(qx, mk, mv)


def _out_proj_kernel(x_ref, mix_ref, xo_ref, w1_ref, w2_ref, o_ref):
    o_ref[...] = x_ref[...] + _dot(mix_ref[...], w1_ref[...]) + _dot(xo_ref[...], w2_ref[...])


def out_proj(x, mix, xo, w, layer, tm):
    t, d = x.shape
    c = mix.shape[1]
    return pl.pallas_call(
        _out_proj_kernel,
        grid=(t // tm,),
        in_specs=[pl.BlockSpec((tm, d), lambda i: (i, 0)),
                  pl.BlockSpec((tm, c), lambda i: (i, 0)),
                  pl.BlockSpec((tm, XATTN_WIDTH), lambda i: (i, 0)),
                  pl.BlockSpec((None, c, d), lambda i: (layer, 0, 0)),
                  pl.BlockSpec((None, XATTN_WIDTH, d), lambda i: (layer, c // XATTN_WIDTH, 0))],
        out_specs=pl.BlockSpec((tm, d), lambda i: (i, 0)),
        out_shape=jax.ShapeDtypeStruct((t, d), F32),
        compiler_params=_params(("arbitrary",)),
        name="out_proj",
    )(x, mix, xo, w, w)


def _mlp_kernel(x_ref, g_ref, wu_ref, wd_ref, gf_ref, o_ref, xn_ref, *, final_norm):
    f = pl.program_id(1)

    @pl.when(f == 0)
    def _():
        x = x_ref[...]
        xn_ref[...] = _rms(x, g_ref[...]).astype(BF16)
        o_ref[...] = x

    h = jnp.maximum(_dot(xn_ref[...], wu_ref[...]), 0.0)
    o_ref[...] += _dot((h * h).astype(BF16), wd_ref[...])

    if final_norm:
        @pl.when(f == pl.num_programs(1) - 1)
        def _():
            o_ref[...] = _rms(o_ref[...], gf_ref[...])


def mlp(x, g, w_up, w_down, g_final, layer, tm, tf, final_norm):
    t, d = x.shape
    ff = w_up.shape[2]
    return pl.pallas_call(
        functools.partial(_mlp_kernel, final_norm=final_norm),
        grid=(t // tm, ff // tf),
        in_specs=[pl.BlockSpec((tm, d), lambda i, f: (i, 0)),
                  pl.BlockSpec((1, d), lambda i, f: (0, 0)),
                  pl.BlockSpec((None, d, tf), lambda i, f: (layer, 0, f)),
                  pl.BlockSpec((None, tf, d), lambda i, f: (layer, f, 0)),
                  pl.BlockSpec((1, d), lambda i, f: (0, 0))],
        out_specs=pl.BlockSpec((tm, d), lambda i, f: (i, 0)),
        out_shape=jax.ShapeDtypeStruct((t, d), F32),
        scratch_shapes=[pltpu.VMEM((tm, d), BF16)],
        compiler_params=_params(("arbitrary", "arbitrary")),
        name="mlp",
    )(x, g, w_up, w_down, g_final)


def _token_tile(t, pref):
    return pref if t % pref == 0 else t


def kernel(x_prompt, x_sample, cache_mem_k, cache_mem_v, state_conv, cache_fox_k, cache_fox_v, cache_fox_logf, mem_prompt, g_mix, g_mem, w_mem_k, w_mem_v, w_in_conv, conv_w, conv_b, conv_ln_g, conv_ln_b, w_in_fox, b_fox_f, w_out, g_mlp, w_up, w_down, g_final):
    bp, sp, d = x_prompt.shape
    bs, ss, _ = x_sample.shape
    depth = g_mix.shape[0]
    n_mem = mem_prompt.shape[1]
    past = cache_fox_k.shape[2]
    c = MIXER_WIDTH

    groups = {
        "p": dict(y=x_prompt.reshape(bp * sp, d), b=bp, s=sp, tm_conv=512, tm_fox=1024),
        "s": dict(y=x_sample.reshape(bs * ss, d), b=bs, s=ss, tm_conv=bs * ss, tm_fox=bs * ss),
    }
    mem_flat = mem_prompt.reshape(bp * n_mem, d)
    row = lambda a: a.reshape(1, -1)
    w_o, w_u, w_d = w_out.astype(BF16), w_up.astype(BF16), w_down.astype(BF16)

    mem_k_p, mem_v_p, conv_p, conv_s = [], [], [], []
    fk, fv, fl = {"p": [], "s": []}, {"p": [], "s": []}, {"p": [], "s": []}

    for i in range(depth):
        j = i // 2
        w_kv = jnp.concatenate([w_mem_k[i], w_mem_v[i]], axis=1).astype(BF16)
        kv = rms_matmul(mem_flat, row(g_mem[i]), w_kv, XATTN_WIDTH)
        mk_p = kv[:, :XATTN_WIDTH].reshape(bp, n_mem, XATTN_WIDTH)
        mv_p = kv[:, XATTN_WIDTH:].reshape(bp, n_mem, XATTN_WIDTH)
        mem_k_p.append(mk_p.reshape(bp, n_mem, XATTN_HEADS, CHUNK_HEAD))
        mem_v_p.append(mv_p.reshape(bp, n_mem, XATTN_HEADS, CHUNK_HEAD))
        mem = {"p": (mk_p.astype(BF16), mv_p.astype(BF16)),
               "s": (cache_mem_k[i].reshape(bs, n_mem, XATTN_WIDTH).astype(BF16),
                     cache_mem_v[i].reshape(bs, n_mem, XATTN_WIDTH).astype(BF16))}
        if i % 2 == 0:
            w_in = w_in_conv[j].astype(BF16)
        else:
            wf = w_in_fox[j]
            w_in = jnp.concatenate([wf[:, :3 * c].astype(BF16), wf[:, 3 * c + FOX_HEADS:].astype(BF16)], axis=1)
            wf_t = jnp.pad(wf[:, 3 * c:3 * c + FOX_HEADS].astype(BF16).T, ((0, LOGF_ROWS - FOX_HEADS), (0, 0)))
            bf = jnp.zeros((LOGF_ROWS, 1), F32).at[:FOX_HEADS, 0].set(b_fox_f[j])

        for name, grp in groups.items():
            y, b, s = grp["y"], grp["b"], grp["s"]
            if i % 2 == 0:
                if name == "p":
                    hist = jnp.zeros((b, CONV_HIST, c), F32)
                else:
                    hist = jnp.concatenate([jnp.zeros((b, CONV_HIST - CONV_STATE, c), F32), state_conv[j]], axis=1)
                mix, qx, tail = conv_front(y, row(g_mix[i]), w_in, hist, conv_w[j], row(conv_b[j]),
                                           row(conv_ln_g[j]), row(conv_ln_b[j]), grp["tm_conv"], XATTN_WIDTH)
                (conv_p if name == "p" else conv_s).append(tail[:, CONV_HIST - CONV_STATE:])
            else:
                q, k_f, k_b, v_f, v_b, qx, lf_t = fox_inproj(
                    y, row(g_mix[i]), w_in, wf_t, bf, b, grp["tm_fox"], XATTN_WIDTH, transposed=(name == "p"),
                    q_scale=ATTN_SCALE * LOG2E if name == "p" else ATTN_SCALE)
                fk[name].append(k_f.transpose(0, 2, 1, 3))
                fv[name].append(v_f.transpose(0, 2, 1, 3))
                fl[name].append(lf_t[:FOX_HEADS].reshape(FOX_HEADS, b, s).transpose(1, 2, 0))
                k_b = k_b.reshape(b, s, c)
                if name == "p":
                    nc = neg_cumsum(lf_t, b)
                    mix = fox_attention(q, k_b, key_bias(nc, b, s, 512), v_b, 512, 3)
                else:
                    q = q.reshape(b, s, c)
                    v_b = v_b.reshape(b, s, c)
                    total = past + s
                    padded = -(-total // LANE) * LANE
                    lf_old = jnp.zeros((b, LOGF_ROWS, past), F32).at[:, :FOX_HEADS].set(
                        cache_fox_logf[j].astype(F32).transpose(0, 2, 1))
                    lf_all = jnp.concatenate(
                        [lf_old, lf_t.reshape(LOGF_ROWS, b, s).transpose(1, 0, 2),
                         jnp.zeros((b, LOGF_ROWS, padded - total), F32)], axis=2)
                    lf_all = lf_all.transpose(1, 0, 2).reshape(LOGF_ROWS, b * padded)
                    nc = neg_cumsum(lf_all, b)
                    nc = nc.reshape(LOGF_ROWS, b, padded).transpose(1, 0, 2).reshape(b * LOGF_ROWS, 1, padded)
                    mix = fox_decode_attention(q, cache_fox_k[j].transpose(0, 2, 1, 3),
                                               cache_fox_v[j].transpose(0, 2, 1, 3), k_b, v_b, nc)
            qx = qx.reshape(b, s, XATTN_WIDTH)
            xo = mem_attention(qx, mem[name][0], mem[name][1], _token_tile(s, 1024))
            y = out_proj(y, mix.reshape(b * s, c), xo.reshape(b * s, XATTN_WIDTH), w_o, i, _token_tile(b * s, 512))
            y = mlp(y, row(g_mlp[i]), w_u, w_d, row(g_final), i, _token_tile(b * s, 512), 1024,
                    final_norm=(i == depth - 1))
            grp["y"] = y

    y_prompt = groups["p"]["y"].reshape(bp, sp, d)
    y_sample = groups["s"]["y"].reshape(bs, ss, d)
    return (y_prompt, y_sample, jnp.stack(mem_k_p), jnp.stack(mem_v_p), jnp.stack(conv_p), jnp.stack(conv_s),
            jnp.stack(fk["p"]), jnp.stack(fv["p"]), jnp.stack(fl["p"]),
            jnp.stack(fk["s"]), jnp.stack(fv["s"]), jnp.stack(fl["s"]))
```

```python
import functools

import jax
import jax.numpy as jnp
from jax import lax
from jax.experimental import pallas as pl
from jax.experimental.pallas import tpu as pltpu

F32 = jnp.float32
BF16 = jnp.bfloat16

D_MODEL = 2048
CHUNK_HEAD = 128
XATTN_HEADS = 4
XATTN_WIDTH = XATTN_HEADS * CHUNK_HEAD
MIXER_WIDTH = D_MODEL - XATTN_WIDTH
FOX_HEADS = MIXER_WIDTH // CHUNK_HEAD
CONV_WIDTH = 31
CONV_STATE = CONV_WIDTH - 1
CONV_HIST = 32
D_FF = 4 * D_MODEL
EPS = 1e-6
ATTN_SCALE = CHUNK_HEAD ** -0.5
LOG2E = 1.4426950408889634
LOGF_ROWS = 16
LANE = 128
SUBLANE = 8
BIAS_PARTS = 3

VMEM_LIMIT = 56 * 1024 * 1024


def _params(sem):
    return pltpu.CompilerParams(dimension_semantics=sem, vmem_limit_bytes=VMEM_LIMIT)


def _rms(x, g):
    ms = jnp.mean(x * x, axis=-1, keepdims=True)
    return x * lax.rsqrt(ms + EPS) * g


def _sigmoid(x):
    return 1.0 / (1.0 + jnp.exp(-x))


def _dot(a, b):
    return jnp.dot(a, b, preferred_element_type=F32)


def _dot_nt(a, b):
    return lax.dot_general(a, b, (((1,), (1,)), ((), ())), preferred_element_type=F32)


def _rms_matmul_kernel(x_ref, g_ref, w_ref, o_ref):
    xn = _rms(x_ref[...], g_ref[...]).astype(BF16)
    o_ref[...] = _dot(xn, w_ref[...])


def rms_matmul(x, g, w, tn):
    t, d = x.shape
    n = w.shape[1]
    return pl.pallas_call(
        _rms_matmul_kernel,
        grid=(n // tn,),
        in_specs=[pl.BlockSpec((t, d), lambda j: (0, 0)),
                  pl.BlockSpec((1, d), lambda j: (0, 0)),
                  pl.BlockSpec((d, tn), lambda j: (0, j))],
        out_specs=pl.BlockSpec((t, tn), lambda j: (0, j)),
        out_shape=jax.ShapeDtypeStruct((t, n), F32),
        compiler_params=_params(("arbitrary",)),
        name="rms_matmul",
    )(x, g, w)


def _conv_front_kernel(x_ref, g_ref, wa_ref, wg_ref, wq_ref, hist_ref, cw_ref, cb_ref, lg_ref, lb_ref,
                       mix_ref, qx_ref, tail_ref, xn_ref, win_a, win_b, carry_ref, conv_ref,
                       *, n_glu, blk_per_seq, tr):
    i = pl.program_id(0)
    j = pl.program_id(1)
    nseq, _, tn = win_a.shape
    rows = win_a.shape[1] - CONV_HIST
    first_block = (i % blk_per_seq) == 0
    windows = (win_a, win_b)

    def project(p):
        xn = xn_ref[...]
        u = _dot(xn, wa_ref[...]) * _sigmoid(_dot(xn, wg_ref[...]))
        win = windows[p % 2]
        cols = slice(p * tn, (p + 1) * tn)
        for sq in range(nseq):
            win[sq, 0:CONV_HIST, :] = jnp.where(first_block, hist_ref[sq, :, cols], carry_ref[sq, :, cols])
            win[sq, CONV_HIST:CONV_HIST + rows, :] = u[sq * rows:(sq + 1) * rows]

    def convolve(p):
        win = windows[p % 2]
        off = CONV_HIST - CONV_STATE
        span = tr + CONV_HIST
        for sq in range(nseq):
            for c0 in range(0, tn, LANE):
                ch = slice(p * tn + c0, p * tn + c0 + LANE)
                for r0 in range(0, rows, tr):
                    window = win[sq, r0:r0 + span, c0:c0 + LANE]
                    acc = jnp.zeros((tr, LANE), F32) + cb_ref[:, ch]
                    for res in range(SUBLANE):
                        shifted = pltpu.roll(window, span - res, axis=0) if res else window
                        for k in range(CONV_WIDTH):
                            if (off + k) % SUBLANE == res:
                                a0 = off + k - res
                                acc = acc + cw_ref[k:k + 1, ch] * shifted[a0:a0 + tr]
                    conv_ref[sq * rows + r0:sq * rows + r0 + tr, ch] = acc
            tail = win[sq, rows:rows + CONV_HIST, :]
            carry_ref[sq, :, p * tn:(p + 1) * tn] = tail
            tail_ref[sq, :, p * tn:(p + 1) * tn] = tail

    @pl.when(jnp.logical_and(i == 0, j == 0))
    def _():
        carry_ref[...] = jnp.zeros(carry_ref.shape, F32)

    @pl.when(j == 0)
    def _():
        xn_ref[...] = _rms(x_ref[...], g_ref[...]).astype(BF16)
        project(0)

    for p in range(1, n_glu):
        @pl.when(j == p)
        def _():
            convolve(p - 1)
            project(p)

    @pl.when(j == n_glu)
    def _():
        convolve(n_glu - 1)
        qx_ref[...] = _dot(xn_ref[...], wq_ref[...]).astype(BF16)
        y = conv_ref[...]
        yc = y - jnp.mean(y, axis=-1, keepdims=True)
        yn = yc * lax.rsqrt(jnp.mean(yc * yc, axis=-1, keepdims=True) + EPS)
        yn = yn * lg_ref[...] + lb_ref[...]
        mix_ref[...] = (yn * _sigmoid(yn)).astype(BF16)


def conv_front(x, g, w, hist, conv_w, conv_b, ln_g, ln_b, tm, tn):
    t, d = x.shape
    nseq_total, _, c = hist.shape
    s = t // nseq_total
    n_glu = c // tn
    last = n_glu - 1
    rows = min(tm, s)
    seq_per_blk = tm // rows
    blk_per_seq = s // rows
    per_seq = pl.BlockSpec((seq_per_blk, CONV_HIST, c), lambda i, j: (i // blk_per_seq, 0, 0))
    const = lambda shape: pl.BlockSpec(shape, lambda i, j: (0, 0))
    return pl.pallas_call(
        functools.partial(_conv_front_kernel, n_glu=n_glu, blk_per_seq=blk_per_seq, tr=min(rows, 64)),
        grid=(t // tm, n_glu + 1),
        in_specs=[pl.BlockSpec((tm, d), lambda i, j: (i, 0)),
                  const((1, d)),
                  pl.BlockSpec((d, tn), lambda i, j: (0, jnp.minimum(j, last))),
                  pl.BlockSpec((d, tn), lambda i, j: (0, n_glu + jnp.minimum(j, last))),
                  pl.BlockSpec((d, XATTN_WIDTH), lambda i, j: (0, 2 * c // XATTN_WIDTH)),
                  per_seq,
                  const((CONV_WIDTH, c)), const((1, c)), const((1, c)), const((1, c))],
        out_specs=[pl.BlockSpec((tm, c), lambda i, j: (i, 0)),
                   pl.BlockSpec((tm, XATTN_WIDTH), lambda i, j: (i, 0)),
                   per_seq],
        out_shape=[jax.ShapeDtypeStruct((t, c), BF16),
                   jax.ShapeDtypeStruct((t, XATTN_WIDTH), BF16),
                   jax.ShapeDtypeStruct((nseq_total, CONV_HIST, c), F32)],
        scratch_shapes=[pltpu.VMEM((tm, d), BF16),
                        pltpu.VMEM((seq_per_blk, CONV_HIST + rows, tn), F32),
                        pltpu.VMEM((seq_per_blk, CONV_HIST + rows, tn), F32),
                        pltpu.VMEM((seq_per_blk, CONV_HIST, c), F32),
                        pltpu.VMEM((tm, c), F32)],
        compiler_params=_params(("arbitrary", "arbitrary")),
        name="conv_front",
    )(x, g, w, w, w, hist, conv_w, conv_b, ln_g, ln_b)


def _fox_inproj_kernel(x_ref, g_ref, w_ref, wf_ref, bf_ref,
                       q_ref, kf_ref, kb_ref, vf_ref, vb_ref, qx_ref, lf_ref, xn_ref,
                       *, n_part, transposed, q_scale):
    j = pl.program_id(1)

    def oriented(a):
        return a.T if transposed else a

    @pl.when(j == 0)
    def _():
        xn = _rms(x_ref[...], g_ref[...]).astype(BF16)
        xn_ref[...] = xn
        f = _dot_nt(wf_ref[...], xn) + bf_ref[...]
        lf_ref[...] = jnp.minimum(f, 0.0) - jnp.log1p(jnp.exp(-jnp.abs(f)))

    z = _dot_nt(xn_ref[...], w_ref[...])

    @pl.when(j < n_part)
    def _():
        q_ref[...] = oriented(z * q_scale).astype(BF16)

    def store_heads(ref):
        nseq, nheads, rows, _ = ref.shape
        for sq in range(nseq):
            for hh in range(nheads):
                ref[sq, hh] = z[sq * rows:(sq + 1) * rows, hh * CHUNK_HEAD:(hh + 1) * CHUNK_HEAD]

    @pl.when(jnp.logical_and(j >= n_part, j < 2 * n_part))
    def _():
        store_heads(kf_ref)
        kb_ref[...] = z.astype(BF16)

    @pl.when(jnp.logical_and(j >= 2 * n_part, j < 3 * n_part))
    def _():
        store_heads(vf_ref)
        vb_ref[...] = oriented(z).astype(BF16)

    @pl.when(j == 3 * n_part)
    def _():
        qx_ref[...] = z.astype(BF16)


def fox_inproj(x, g, w, wf_t, bf, nseq_total, tm, tn, transposed, q_scale):
    t, d = x.shape
    c = MIXER_WIDTH
    s = t // nseq_total
    n_part = c // tn
    assert tn == XATTN_WIDTH
    rows = min(tm, s)
    seq_per_blk = tm // rows
    blk_per_seq = s // rows

    def part(p):
        return lambda i, j: (i, jnp.clip(j - p * n_part, 0, n_part - 1))

    def part_t(p):
        return lambda i, j: (jnp.clip(j - p * n_part, 0, n_part - 1), i)

    def head_major(p):
        return pl.BlockSpec(
            (seq_per_blk, tn // CHUNK_HEAD, rows, CHUNK_HEAD),
            lambda i, j: (i // blk_per_seq, jnp.clip(j - p * n_part, 0, n_part - 1), i % blk_per_seq, 0))

    if transposed:
        qv_shape = jax.ShapeDtypeStruct((c, t), BF16)
        q_spec, v_spec = pl.BlockSpec((tn, tm), part_t(0)), pl.BlockSpec((tn, tm), part_t(2))
    else:
        qv_shape = jax.ShapeDtypeStruct((t, c), BF16)
        q_spec, v_spec = pl.BlockSpec((tm, tn), part(0)), pl.BlockSpec((tm, tn), part(2))
    kv_shape = jax.ShapeDtypeStruct((nseq_total, FOX_HEADS, s, CHUNK_HEAD), F32)

    return pl.pallas_call(
        functools.partial(_fox_inproj_kernel, n_part=n_part, transposed=transposed, q_scale=q_scale),
        grid=(t // tm, 3 * n_part + 1),
        in_specs=[pl.BlockSpec((tm, d), lambda i, j: (i, 0)),
                  pl.BlockSpec((1, d), lambda i, j: (0, 0)),
                  pl.BlockSpec((tn, d), lambda i, j: (j, 0)),
                  pl.BlockSpec((LOGF_ROWS, d), lambda i, j: (0, 0)),
                  pl.BlockSpec((LOGF_ROWS, 1), lambda i, j: (0, 0))],
        out_specs=[q_spec,
                   head_major(1),
                   pl.BlockSpec((tm, tn), part(1)),
                   head_major(2),
                   v_spec,
                   pl.BlockSpec((tm, XATTN_WIDTH), lambda i, j: (i, 0)),
                   pl.BlockSpec((LOGF_ROWS, tm), lambda i, j: (0, i))],
        out_shape=[qv_shape,
                   kv_shape,
                   jax.ShapeDtypeStruct((t, c), BF16),
                   kv_shape,
                   qv_shape,
                   jax.ShapeDtypeStruct((t, XATTN_WIDTH), BF16),
                   jax.ShapeDtypeStruct((LOGF_ROWS, t), F32)],
        scratch_shapes=[pltpu.VMEM((tm, d), BF16)],
        compiler_params=_params(("arbitrary", "arbitrary")),
        name="fox_inproj",
    )(x, g, w, wf_t, bf)


def _neg_cumsum_kernel(x_ref, o_ref):
    x = x_ref[...]
    n = x.shape[-1]
    lane = lax.broadcasted_iota(jnp.int32, x.shape, x.ndim - 1)
    d = 1
    while d < n:
        x = x + jnp.where(lane >= d, pltpu.roll(x, d, axis=x.ndim - 1), 0.0)
        d *= 2
    o_ref[...] = -x


def neg_cumsum(x, nseq):
    r, total = x.shape
    seg = total // nseq
    return pl.pallas_call(
        _neg_cumsum_kernel,
        grid=(nseq,),
        in_specs=[pl.BlockSpec((r, seg), lambda b: (0, b))],
        out_specs=pl.BlockSpec((r, seg), lambda b: (0, b)),
        out_shape=jax.ShapeDtypeStruct((r, total), F32),
        compiler_params=_params(("arbitrary",)),
        name="neg_cumsum",
    )(x)


def _key_bias_kernel(nc_ref, place_ref, o_ref):
    x = nc_ref[...] * LOG2E
    hi = x.astype(BF16)
    rest = x - hi.astype(F32)
    mid = rest.astype(BF16)
    lo = (rest - mid.astype(F32)).astype(BF16)
    parts = jnp.concatenate([hi, mid, lo], axis=0)
    tiles = lax.dot_general(parts, place_ref[...], (((0,), (0,)), ((), ())), preferred_element_type=F32)
    o_ref[0] = tiles.astype(BF16)


def key_bias(nc, b, s, ts):
    part = jnp.arange(BIAS_PARTS * LOGF_ROWS) // LOGF_ROWS
    head = jnp.arange(BIAS_PARTS * LOGF_ROWS) % LOGF_ROWS
    lane = jnp.arange(MIXER_WIDTH)
    place = ((lane[None, :] == (head * CHUNK_HEAD + part)[:, None]) & (head < FOX_HEADS)[:, None]).astype(BF16)
    return pl.pallas_call(
        _key_bias_kernel,
        grid=(b, s // ts),
        in_specs=[pl.BlockSpec((LOGF_ROWS, ts), lambda bi, i: (0, bi * (s // ts) + i)),
                  pl.BlockSpec((BIAS_PARTS * LOGF_ROWS, MIXER_WIDTH), lambda bi, i: (0, 0))],
        out_specs=pl.BlockSpec((1, ts, MIXER_WIDTH), lambda bi, i: (bi, i, 0)),
        out_shape=jax.ShapeDtypeStruct((b, s, MIXER_WIDTH), BF16),
        compiler_params=_params(("arbitrary", "arbitrary")),
        name="key_bias",
    )(nc, place)


def _fox_attn_kernel(qt_ref, k_ref, kb_ref, vt_ref, o_ref, qa_ref, sa_ref, sb_ref, m_ref, l_ref, acc_ref,
                     *, blk, hp):
    i = pl.program_id(2)
    heads = [slice(hh * CHUNK_HEAD, (hh + 1) * CHUNK_HEAD) for hh in range(hp)]

    ones = (lax.broadcasted_iota(jnp.int32, (CHUNK_HEAD, blk), 0) < BIAS_PARTS).astype(BF16)
    for hh, sl in enumerate(heads):
        qa_ref[hh, 0:CHUNK_HEAD, :] = qt_ref[sl, :]
        qa_ref[hh, CHUNK_HEAD:2 * CHUNK_HEAD, :] = ones
    m_ref[...] = jnp.full(m_ref.shape, -jnp.inf, F32)
    l_ref[...] = jnp.zeros(l_ref.shape, F32)
    acc_ref[...] = jnp.zeros(acc_ref.shape, F32)

    def scores(j, dst):
        start = pl.multiple_of(j * blk, blk)
        for hh, sl in enumerate(heads):
            ka = jnp.concatenate([k_ref[0, pl.ds(start, blk), sl], kb_ref[0, pl.ds(start, blk), sl]], axis=1)
            dst[hh] = _dot(ka, qa_ref[hh])

    def absorb(j, src, diagonal):
        start = pl.multiple_of(j * blk, blk)
        for hh, sl in enumerate(heads):
            s = src[hh]
            if diagonal:
                key = lax.broadcasted_iota(jnp.int32, s.shape, 0)
                qry = lax.broadcasted_iota(jnp.int32, s.shape, 1)
                s = jnp.where(key <= qry, s, -jnp.inf)
            m_prev = m_ref[hh]
            m_new = jnp.maximum(m_prev, jnp.max(s, axis=0, keepdims=True))
            alpha = jnp.exp2(m_prev - m_new)
            p = jnp.exp2(s - m_new)
            l_ref[hh] = alpha * l_ref[hh] + jnp.sum(p, axis=0, keepdims=True)
            acc_ref[hh] = alpha * acc_ref[hh] + _dot(vt_ref[sl, pl.ds(start, blk)], p.astype(BF16))
            m_ref[hh] = m_new

    scores(0, sa_ref)

    def pair(p, carry):
        j = 2 * p
        scores(j + 1, sb_ref)
        absorb(j, sa_ref, False)
        scores(j + 2, sa_ref)
        absorb(j + 1, sb_ref, False)
        return carry

    lax.fori_loop(0, jnp.right_shift(i, 1), pair, 0)

    @pl.when(jnp.bitwise_and(i, 1) == 0)
    def _():
        absorb(i, sa_ref, True)

    @pl.when(jnp.bitwise_and(i, 1) == 1)
    def _():
        scores(i, sb_ref)
        absorb(i - 1, sa_ref, False)
        absorb(i, sb_ref, True)

    for hh in range(hp):
        out_t = acc_ref[hh] * (1.0 / l_ref[hh])
        o_ref[0, :, hh * CHUNK_HEAD:(hh + 1) * CHUNK_HEAD] = out_t.T.astype(BF16)


def fox_attention(qt, k, kb, vt, blk, hp):
    b, s, w = k.shape
    h = w // CHUNK_HEAD
    wide = hp * CHUNK_HEAD
    nq = s // blk
    return pl.pallas_call(
        functools.partial(_fox_attn_kernel, blk=blk, hp=hp),
        grid=(b, h // hp, nq),
        in_specs=[pl.BlockSpec((wide, blk), lambda bi, hi, i: (hi, bi * nq + i)),
                  pl.BlockSpec((1, s, wide), lambda bi, hi, i: (bi, 0, hi)),
                  pl.BlockSpec((1, s, wide), lambda bi, hi, i: (bi, 0, hi)),
                  pl.BlockSpec((wide, s), lambda bi, hi, i: (hi, bi))],
        out_specs=pl.BlockSpec((1, blk, wide), lambda bi, hi, i: (bi, i, hi)),
        out_shape=jax.ShapeDtypeStruct((b, s, w), BF16),
        scratch_shapes=[pltpu.VMEM((hp, 2 * CHUNK_HEAD, blk), BF16),
                        pltpu.VMEM((hp, blk, blk), F32), pltpu.VMEM((hp, blk, blk), F32),
                        pltpu.VMEM((hp, 1, blk), F32), pltpu.VMEM((hp, 1, blk), F32),
                        pltpu.VMEM((hp, CHUNK_HEAD, blk), F32)],
        compiler_params=_params(("arbitrary", "arbitrary", "arbitrary")),
        name="fox_attention",
    )(qt, k, kb, vt)


def _fox_decode_kernel(q_ref, kc_ref, vc_ref, kn_ref, vn_ref, nc_ref, o_ref, *, past, t, heads):
    row = lax.broadcasted_iota(jnp.int32, (t, t), 0)
    col = lax.broadcasted_iota(jnp.int32, (t, t), 1)
    for h in range(heads):
        sl = slice(h * CHUNK_HEAD, (h + 1) * CHUNK_HEAD)
        q = q_ref[0, :, sl]
        s1 = _dot_nt(q, kc_ref[0, h].astype(BF16)) + nc_ref[h, :, 0:past]
        s2 = _dot_nt(q, kn_ref[0, :, sl]) + nc_ref[h, :, past:past + t]
        s2 = jnp.where(col <= row, s2, -jnp.inf)
        m = jnp.maximum(jnp.max(s1, axis=-1, keepdims=True), jnp.max(s2, axis=-1, keepdims=True))
        p1 = jnp.exp(s1 - m)
        p2 = jnp.exp(s2 - m)
        l = jnp.sum(p1, axis=-1, keepdims=True) + jnp.sum(p2, axis=-1, keepdims=True)
        o = _dot(p1.astype(BF16), vc_ref[0, h].astype(BF16)) + _dot(p2.astype(BF16), vn_ref[0, :, sl])
        o_ref[0, :, sl] = (o / l).astype(BF16)


def fox_decode_attention(q, kc, vc, kn, vn, nc):
    b, t, w = q.shape
    _, h, past, dh = kc.shape
    npad = nc.shape[-1]
    new = pl.BlockSpec((1, t, w), lambda bi: (bi, 0, 0))
    old = pl.BlockSpec((1, h, past, dh), lambda bi: (bi, 0, 0, 0))
    return pl.pallas_call(
        functools.partial(_fox_decode_kernel, past=past, t=t, heads=h),
        grid=(b,),
        in_specs=[new, old, old, new, new,
                  pl.BlockSpec((LOGF_ROWS, 1, npad), lambda bi: (bi, 0, 0))],
        out_specs=new,
        out_shape=jax.ShapeDtypeStruct((b, t, w), BF16),
        compiler_params=_params(("arbitrary",)),
        name="fox_decode_attention",
    )(q, kc, vc, kn, vn, nc)


def _mem_attn_kernel(q_ref, mk_ref, mv_ref, o_ref):
    for h in range(XATTN_HEADS):
        sl = slice(h * CHUNK_HEAD, (h + 1) * CHUNK_HEAD)
        s = _dot_nt(q_ref[0, :, sl], mk_ref[0, :, sl]) * ATTN_SCALE
        p = jnp.exp(s - jnp.max(s, axis=-1, keepdims=True))
        l = jnp.sum(p, axis=-1, keepdims=True)
        o = _dot(p.astype(BF16), mv_ref[0, :, sl])
        o_ref[0, :, sl] = (o / l).astype(BF16)


def mem_attention(qx, mk, mv, tm):
    b, t, w = qx.shape
    nm = mk.shape[1]
    return pl.pallas_call(
        _mem_attn_kernel,
        grid=(b, t // tm),
        in_specs=[pl.BlockSpec((1, tm, w), lambda bi, i: (bi, i, 0)),
                  pl.BlockSpec((1, nm, w), lambda bi, i: (bi, 0, 0)),
                  pl.BlockSpec((1, nm, w), lambda bi, i: (bi, 0, 0))],
        out_specs=pl.BlockSpec((1, tm, w), lambda bi, i: (bi, i, 0)),
        out_shape=jax.ShapeDtypeStruct((b, t, w), BF16),
        compiler_params=_params(("arbitrary", "arbitrary")),
        name="mem_attention",
    )(qx, mk, mv)


def _out_proj_kernel(x_ref, mix_ref, xo_ref, w1_ref, w2_ref, o_ref):
    o_ref[...] = x_ref[...] + _dot(mix_ref[...], w1_ref[...]) + _dot(xo_ref[...], w2_ref[...])


def out_proj(x, mix, xo, w, layer, tm):
    t, d = x.shape
    c = mix.shape[1]
    return pl.pallas_call(
        _out_proj_kernel,
        grid=(t // tm,),
        in_specs=[pl.BlockSpec((tm, d), lambda i: (i, 0)),
                  pl.BlockSpec((tm, c), lambda i: (i, 0)),
                  pl.BlockSpec((tm, XATTN_WIDTH), lambda i: (i, 0)),
                  pl.BlockSpec((None, c, d), lambda i: (layer, 0, 0)),
                  pl.BlockSpec((None, XATTN_WIDTH, d), lambda i: (layer, c // XATTN_WIDTH, 0))],
        out_specs=pl.BlockSpec((tm, d), lambda i: (i, 0)),
        out_shape=jax.ShapeDtypeStruct((t, d), F32),
        compiler_params=_params(("arbitrary",)),
        name="out_proj",
    )(x, mix, xo, w, w)


def _mlp_kernel(x_ref, g_ref, wu_ref, wd_ref, gf_ref, o_ref, xn_ref, *, final_norm):
    f = pl.program_id(1)

    @pl.when(f == 0)
    def _():
        x = x_ref[...]
        xn_ref[...] = _rms(x, g_ref[...]).astype(BF16)
        o_ref[...] = x

    h = jnp.maximum(_dot(xn_ref[...], wu_ref[...]), 0.0)
    o_ref[...] += _dot((h * h).astype(BF16), wd_ref[...])

    if final_norm:
        @pl.when(f == pl.num_programs(1) - 1)
        def _():
            o_ref[...] = _rms(o_ref[...], gf_ref[...])


def mlp(x, g, w_up, w_down, g_final, layer, tm, tf, final_norm):
    t, d = x.shape
    ff = w_up.shape[2]
    return pl.pallas_call(
        functools.partial(_mlp_kernel, final_norm=final_norm),
        grid=(t // tm, ff // tf),
        in_specs=[pl.BlockSpec((tm, d), lambda i, f: (i, 0)),
                  pl.BlockSpec((1, d), lambda i, f: (0, 0)),
                  pl.BlockSpec((None, d, tf), lambda i, f: (layer, 0, f)),
                  pl.BlockSpec((None, tf, d), lambda i, f: (layer, f, 0)),
                  pl.BlockSpec((1, d), lambda i, f: (0, 0))],
        out_specs=pl.BlockSpec((tm, d), lambda i, f: (i, 0)),
        out_shape=jax.ShapeDtypeStruct((t, d), F32),
        scratch_shapes=[pltpu.VMEM((tm, d), BF16)],
        compiler_params=_params(("arbitrary", "arbitrary")),
        name="mlp",
    )(x, g, w_up, w_down, g_final)


def _token_tile(t, pref):
    return pref if t % pref == 0 else t


def kernel(x_prompt, x_sample, cache_mem_k, cache_mem_v, state_conv, cache_fox_k, cache_fox_v, cache_fox_logf, mem_prompt, g_mix, g_mem, w_mem_k, w_mem_v, w_in_conv, conv_w, conv_b, conv_ln_g, conv_ln_b, w_in_fox, b_fox_f, w_out, g_mlp, w_up, w_down, g_final):
    bp, sp, d = x_prompt.shape
    bs, ss, _ = x_sample.shape
    depth = g_mix.shape[0]
    n_mem = mem_prompt.shape[1]
    past = cache_fox_k.shape[2]
    c = MIXER_WIDTH

    groups = {
        "p": dict(y=x_prompt.reshape(bp * sp, d), b=bp, s=sp, tm_conv=1024, tm_fox=1024),
        "s": dict(y=x_sample.reshape(bs * ss, d), b=bs, s=ss, tm_conv=bs * ss, tm_fox=bs * ss),
    }
    mem_flat = mem_prompt.reshape(bp * n_mem, d)
    row = lambda a: a.reshape(1, -1)
    w_o, w_u, w_d = w_out.astype(BF16), w_up.astype(BF16), w_down.astype(BF16)

    mem_k_p, mem_v_p, conv_p, conv_s = [], [], [], []
    fk, fv, fl = {"p": [], "s": []}, {"p": [], "s": []}, {"p": [], "s": []}

    for i in range(depth):
        j = i // 2
        w_kv = jnp.concatenate([w_mem_k[i], w_mem_v[i]], axis=1).astype(BF16)
        kv = rms_matmul(mem_flat, row(g_mem[i]), w_kv, XATTN_WIDTH)
        mk_p = kv[:, :XATTN_WIDTH].reshape(bp, n_mem, XATTN_WIDTH)
        mv_p = kv[:, XATTN_WIDTH:].reshape(bp, n_mem, XATTN_WIDTH)
        mem_k_p.append(mk_p.reshape(bp, n_mem, XATTN_HEADS, CHUNK_HEAD))
        mem_v_p.append(mv_p.reshape(bp, n_mem, XATTN_HEADS, CHUNK_HEAD))
        mem = {"p": (mk_p.astype(BF16), mv_p.astype(BF16)),
               "s": (cache_mem_k[i].reshape(bs, n_mem, XATTN_WIDTH).astype(BF16),
                     cache_mem_v[i].reshape(bs, n_mem, XATTN_WIDTH).astype(BF16))}
        if i % 2 == 0:
            w_in = w_in_conv[j].astype(BF16)
        else:
            wf = w_in_fox[j]
            wf_rows = wf.T.astype(BF16)
            w_in = jnp.concatenate([wf_rows[:3 * c], wf_rows[3 * c + FOX_HEADS:]], axis=0)
            wf_t = jnp.pad(wf_rows[3 * c:3 * c + FOX_HEADS], ((0, LOGF_ROWS - FOX_HEADS), (0, 0)))
            bf = jnp.zeros((LOGF_ROWS, 1), F32).at[:FOX_HEADS, 0].set(b_fox_f[j])

        for name, grp in groups.items():
            y, b, s = grp["y"], grp["b"], grp["s"]
            if i % 2 == 0:
                if name == "p":
                    hist = jnp.zeros((b, CONV_HIST, c), F32)
                else:
                    hist = jnp.concatenate([jnp.zeros((b, CONV_HIST - CONV_STATE, c), F32), state_conv[j]], axis=1)
                mix, qx, tail = conv_front(y, row(g_mix[i]), w_in, hist, conv_w[j], row(conv_b[j]),
                                           row(conv_ln_g[j]), row(conv_ln_b[j]), grp["tm_conv"], XATTN_WIDTH)
                (conv_p if name == "p" else conv_s).append(tail[:, CONV_HIST - CONV_STATE:])
            else:
                q, k_f, k_b, v_f, v_b, qx, lf_t = fox_inproj(
                    y, row(g_mix[i]), w_in, wf_t, bf, b, grp["tm_fox"], XATTN_WIDTH, transposed=(name == "p"),
                    q_scale=ATTN_SCALE * LOG2E if name == "p" else ATTN_SCALE)
                fk[name].append(k_f.transpose(0, 2, 1, 3))
                fv[name].append(v_f.transpose(0, 2, 1, 3))
                fl[name].append(lf_t[:FOX_HEADS].reshape(FOX_HEADS, b, s).transpose(1, 2, 0))
                k_b = k_b.reshape(b, s, c)
                if name == "p":
                    nc = neg_cumsum(lf_t, b)
                    mix = fox_attention(q, k_b, key_bias(nc, b, s, 512), v_b, 512, 3)
                else:
                    q = q.reshape(b, s, c)
                    v_b = v_b.reshape(b, s, c)
                    total = past + s
                    padded = -(-total // LANE) * LANE
                    lf_old = jnp.zeros((b, LOGF_ROWS, past), F32).at[:, :FOX_HEADS].set(
                        cache_fox_logf[j].astype(F32).transpose(0, 2, 1))
                    lf_all = jnp.concatenate(
                        [lf_old, lf_t.reshape(LOGF_ROWS, b, s).transpose(1, 0, 2),
                         jnp.zeros((b, LOGF_ROWS, padded - total), F32)], axis=2)
                    lf_all = lf_all.transpose(1, 0, 2).reshape(LOGF_ROWS, b * padded)
                    nc = neg_cumsum(lf_all, b)
                    nc = nc.reshape(LOGF_ROWS, b, padded).transpose(1, 0, 2).reshape(b * LOGF_ROWS, 1, padded)
                    mix = fox_decode_attention(q, cache_fox_k[j].transpose(0, 2, 1, 3),
                                               cache_fox_v[j].transpose(0, 2, 1, 3), k_b, v_b, nc)
            qx = qx.reshape(b, s, XATTN_WIDTH)
            xo = mem_attention(qx, mem[name][0], mem[name][1], _token_tile(s, 1024))
            y = out_proj(y, mix.reshape(b * s, c), xo.reshape(b * s, XATTN_WIDTH), w_o, i, _token_tile(b * s, 512))
            mlp_tm, mlp_tf = (1024, 512) if i == 0 else (512, 1024)
            y = mlp(y, row(g_mlp[i]), w_u, w_d, row(g_final), i, _token_tile(b * s, mlp_tm), mlp_tf,
                    final_norm=(i == depth - 1))
            grp["y"] = y

    y_prompt = groups["p"]["y"].reshape(bp, sp, d)
    y_sample = groups["s"]["y"].reshape(bs, ss, d)
    return (y_prompt, y_sample, jnp.stack(mem_k_p), jnp.stack(mem_v_p), jnp.stack(conv_p), jnp.stack(conv_s),
            jnp.stack(fk["p"]), jnp.stack(fv["p"]), jnp.stack(fl["p"]),
            jnp.stack(fk["s"]), jnp.stack(fv["s"]), jnp.stack(fl["s"]))
```

```python
import functools

import jax
import jax.numpy as jnp
from jax import lax
from jax.experimental import pallas as pl
from jax.experimental.pallas import tpu as pltpu

F32 = jnp.float32
BF16 = jnp.bfloat16

D_MODEL = 2048
CHUNK_HEAD = 128
XATTN_HEADS = 4
XATTN_WIDTH = XATTN_HEADS * CHUNK_HEAD
MIXER_WIDTH = D_MODEL - XATTN_WIDTH
FOX_HEADS = MIXER_WIDTH // CHUNK_HEAD
CONV_WIDTH = 31
CONV_STATE = CONV_WIDTH - 1
CONV_HIST = 32
D_FF = 4 * D_MODEL
EPS = 1e-6
ATTN_SCALE = CHUNK_HEAD ** -0.5
LOG2E = 1.4426950408889634
LOGF_ROWS = 16
LANE = 128
SUBLANE = 8
BIAS_PARTS = 3

VMEM_LIMIT = 56 * 1024 * 1024


def _params(sem):
    return pltpu.CompilerParams(dimension_semantics=sem, vmem_limit_bytes=VMEM_LIMIT)


def _rms(x, g):
    ms = jnp.mean(x * x, axis=-1, keepdims=True)
    return x * lax.rsqrt(ms + EPS) * g


def _sigmoid(x):
    return 1.0 / (1.0 + jnp.exp(-x))


def _dot(a, b):
    return jnp.dot(a, b, preferred_element_type=F32)


def _dot_nt(a, b):
    return lax.dot_general(a, b, (((1,), (1,)), ((), ())), preferred_element_type=F32)


def _rms_matmul_kernel(x_ref, g_ref, w_ref, o_ref):
    xn = _rms(x_ref[...], g_ref[...]).astype(BF16)
    o_ref[...] = _dot(xn, w_ref[...])


def rms_matmul(x, g, w, tn):
    t, d = x.shape
    n = w.shape[1]
    return pl.pallas_call(
        _rms_matmul_kernel,
        grid=(n // tn,),
        in_specs=[pl.BlockSpec((t, d), lambda j: (0, 0)),
                  pl.BlockSpec((1, d), lambda j: (0, 0)),
                  pl.BlockSpec((d, tn), lambda j: (0, j))],
        out_specs=pl.BlockSpec((t, tn), lambda j: (0, j)),
        out_shape=jax.ShapeDtypeStruct((t, n), F32),
        compiler_params=_params(("arbitrary",)),
        name="rms_matmul",
    )(x, g, w)


def _conv_front_kernel(x_ref, g_ref, wa_ref, wg_ref, wq_ref, hist_ref, cw_ref, cb_ref, lg_ref, lb_ref,
                       mix_ref, qx_ref, tail_ref, xn_ref, win_a, win_b, carry_ref, conv_ref,
                       *, n_glu, blk_per_seq, tr):
    i = pl.program_id(0)
    j = pl.program_id(1)
    nseq, _, tn = win_a.shape
    rows = win_a.shape[1] - CONV_HIST
    first_block = (i % blk_per_seq) == 0
    windows = (win_a, win_b)

    def project(p):
        xn = xn_ref[...]
        u = _dot(xn, wa_ref[...]) * _sigmoid(_dot(xn, wg_ref[...]))
        win = windows[p % 2]
        cols = slice(p * tn, (p + 1) * tn)
        for sq in range(nseq):
            win[sq, 0:CONV_HIST, :] = jnp.where(first_block, hist_ref[sq, :, cols], carry_ref[sq, :, cols])
            win[sq, CONV_HIST:CONV_HIST + rows, :] = u[sq * rows:(sq + 1) * rows]

    def convolve(p):
        win = windows[p % 2]
        off = CONV_HIST - CONV_STATE
        span = tr + CONV_HIST
        for sq in range(nseq):
            for c0 in range(0, tn, LANE):
                ch = slice(p * tn + c0, p * tn + c0 + LANE)
                for r0 in range(0, rows, tr):
                    window = win[sq, r0:r0 + span, c0:c0 + LANE]
                    acc = jnp.zeros((tr, LANE), F32) + cb_ref[:, ch]
                    for res in range(SUBLANE):
                        shifted = pltpu.roll(window, span - res, axis=0) if res else window
                        for k in range(CONV_WIDTH):
                            if (off + k) % SUBLANE == res:
                                a0 = off + k - res
                                acc = acc + cw_ref[k:k + 1, ch] * shifted[a0:a0 + tr]
                    conv_ref[sq * rows + r0:sq * rows + r0 + tr, ch] = acc
            tail = win[sq, rows:rows + CONV_HIST, :]
            carry_ref[sq, :, p * tn:(p + 1) * tn] = tail
            tail_ref[sq, :, p * tn:(p + 1) * tn] = tail

    @pl.when(jnp.logical_and(i == 0, j == 0))
    def _():
        carry_ref[...] = jnp.zeros(carry_ref.shape, F32)

    @pl.when(j == 0)
    def _():
        xn_ref[...] = _rms(x_ref[...], g_ref[...]).astype(BF16)
        project(0)

    for p in range(1, n_glu):
        @pl.when(j == p)
        def _():
            convolve(p - 1)
            project(p)

    @pl.when(j == n_glu)
    def _():
        convolve(n_glu - 1)
        qx_ref[...] = _dot(xn_ref[...], wq_ref[...]).astype(BF16)
        y = conv_ref[...]
        yc = y - jnp.mean(y, axis=-1, keepdims=True)
        yn = yc * lax.rsqrt(jnp.mean(yc * yc, axis=-1, keepdims=True) + EPS)
        yn = yn * lg_ref[...] + lb_ref[...]
        mix_ref[...] = (yn * _sigmoid(yn)).astype(BF16)


def conv_front(x, g, w, hist, conv_w, conv_b, ln_g, ln_b, tm, tn):
    t, d = x.shape
    nseq_total, _, c = hist.shape
    s = t // nseq_total
    n_glu = c // tn
    last = n_glu - 1
    rows = min(tm, s)
    seq_per_blk = tm // rows
    blk_per_seq = s // rows
    per_seq = pl.BlockSpec((seq_per_blk, CONV_HIST, c), lambda i, j: (i // blk_per_seq, 0, 0))
    const = lambda shape: pl.BlockSpec(shape, lambda i, j: (0, 0))
    return pl.pallas_call(
        functools.partial(_conv_front_kernel, n_glu=n_glu, blk_per_seq=blk_per_seq, tr=min(rows, 64)),
        grid=(t // tm, n_glu + 1),
        in_specs=[pl.BlockSpec((tm, d), lambda i, j: (i, 0)),
                  const((1, d)),
                  pl.BlockSpec((d, tn), lambda i, j: (0, jnp.minimum(j, last))),
                  pl.BlockSpec((d, tn), lambda i, j: (0, n_glu + jnp.minimum(j, last))),
                  pl.BlockSpec((d, XATTN_WIDTH), lambda i, j: (0, 2 * c // XATTN_WIDTH)),
                  per_seq,
                  const((CONV_WIDTH, c)), const((1, c)), const((1, c)), const((1, c))],
        out_specs=[pl.BlockSpec((tm, c), lambda i, j: (i, 0)),
                   pl.BlockSpec((tm, XATTN_WIDTH), lambda i, j: (i, 0)),
                   per_seq],
        out_shape=[jax.ShapeDtypeStruct((t, c), BF16),
                   jax.ShapeDtypeStruct((t, XATTN_WIDTH), BF16),
                   jax.ShapeDtypeStruct((nseq_total, CONV_HIST, c), F32)],
        scratch_shapes=[pltpu.VMEM((tm, d), BF16),
                        pltpu.VMEM((seq_per_blk, CONV_HIST + rows, tn), F32),
                        pltpu.VMEM((seq_per_blk, CONV_HIST + rows, tn), F32),
                        pltpu.VMEM((seq_per_blk, CONV_HIST, c), F32),
                        pltpu.VMEM((tm, c), F32)],
        compiler_params=_params(("arbitrary", "arbitrary")),
        name="conv_front",
    )(x, g, w, w, w, hist, conv_w, conv_b, ln_g, ln_b)


def _fox_inproj_kernel(x_ref, g_ref, w_ref, wq_ref, wf_ref, bf_ref,
                       q_ref, kf_ref, kb_ref, vf_ref, vb_ref, qx_ref, lf_ref, xn_ref,
                       *, n_part, transposed, q_scale):
    j = pl.program_id(1)

    def oriented(a):
        return a.T if transposed else a

    @pl.when(j == 0)
    def _():
        xn = _rms(x_ref[...], g_ref[...]).astype(BF16)
        xn_ref[...] = xn
        f = _dot_nt(wf_ref[...], xn) + bf_ref[...]
        lf_ref[...] = jnp.minimum(f, 0.0) - jnp.log1p(jnp.exp(-jnp.abs(f)))
        qx_ref[...] = _dot_nt(xn, wq_ref[...]).astype(BF16)

    z = _dot_nt(xn_ref[...], w_ref[...])

    @pl.when(j < n_part)
    def _():
        q_ref[...] = oriented(z * q_scale).astype(BF16)

    def store_heads(ref):
        nseq, nheads, rows, _ = ref.shape
        for sq in range(nseq):
            for hh in range(nheads):
                ref[sq, hh] = z[sq * rows:(sq + 1) * rows, hh * CHUNK_HEAD:(hh + 1) * CHUNK_HEAD]

    @pl.when(jnp.logical_and(j >= n_part, j < 2 * n_part))
    def _():
        store_heads(kf_ref)
        kb_ref[...] = z.astype(BF16)

    @pl.when(jnp.logical_and(j >= 2 * n_part, j < 3 * n_part))
    def _():
        store_heads(vf_ref)
        vb_ref[...] = oriented(z).astype(BF16)


def fox_inproj(x, g, w, wq, wf_t, bf, nseq_total, tm, tn, transposed, q_scale):
    t, d = x.shape
    c = MIXER_WIDTH
    s = t // nseq_total
    n_part = c // tn
    assert tn == XATTN_WIDTH
    rows = min(tm, s)
    seq_per_blk = tm // rows
    blk_per_seq = s // rows

    def part(p):
        return lambda i, j: (i, jnp.clip(j - p * n_part, 0, n_part - 1))

    def part_t(p):
        return lambda i, j: (jnp.clip(j - p * n_part, 0, n_part - 1), i)

    def head_major(p):
        return pl.BlockSpec(
            (seq_per_blk, tn // CHUNK_HEAD, rows, CHUNK_HEAD),
            lambda i, j: (i // blk_per_seq, jnp.clip(j - p * n_part, 0, n_part - 1), i % blk_per_seq, 0))

    if transposed:
        qv_shape = jax.ShapeDtypeStruct((c, t), BF16)
        q_spec, v_spec = pl.BlockSpec((tn, tm), part_t(0)), pl.BlockSpec((tn, tm), part_t(2))
    else:
        qv_shape = jax.ShapeDtypeStruct((t, c), BF16)
        q_spec, v_spec = pl.BlockSpec((tm, tn), part(0)), pl.BlockSpec((tm, tn), part(2))
    kv_shape = jax.ShapeDtypeStruct((nseq_total, FOX_HEADS, s, CHUNK_HEAD), F32)

    return pl.pallas_call(
        functools.partial(_fox_inproj_kernel, n_part=n_part, transposed=transposed, q_scale=q_scale),
        grid=(t // tm, 3 * n_part),
        in_specs=[pl.BlockSpec((tm, d), lambda i, j: (i, 0)),
                  pl.BlockSpec((1, d), lambda i, j: (0, 0)),
                  pl.BlockSpec((tn, d), lambda i, j: (j, 0)),
                  pl.BlockSpec((XATTN_WIDTH, d), lambda i, j: (0, 0)),
                  pl.BlockSpec((LOGF_ROWS, d), lambda i, j: (0, 0)),
                  pl.BlockSpec((LOGF_ROWS, 1), lambda i, j: (0, 0))],
        out_specs=[q_spec,
                   head_major(1),
                   pl.BlockSpec((tm, tn), part(1)),
                   head_major(2),
                   v_spec,
                   pl.BlockSpec((tm, XATTN_WIDTH), lambda i, j: (i, 0)),
                   pl.BlockSpec((LOGF_ROWS, tm), lambda i, j: (0, i))],
        out_shape=[qv_shape,
                   kv_shape,
                   jax.ShapeDtypeStruct((t, c), BF16),
                   kv_shape,
                   qv_shape,
                   jax.ShapeDtypeStruct((t, XATTN_WIDTH), BF16),
                   jax.ShapeDtypeStruct((LOGF_ROWS, t), F32)],
        scratch_shapes=[pltpu.VMEM((tm, d), BF16)],
        compiler_params=_params(("arbitrary", "arbitrary")),
        name="fox_inproj",
    )(x, g, w, wq, wf_t, bf)


def _neg_cumsum_kernel(x_ref, o_ref):
    x = x_ref[...]
    n = x.shape[-1]
    lane = lax.broadcasted_iota(jnp.int32, x.shape, x.ndim - 1)
    d = 1
    while d < n:
        x = x + jnp.where(lane >= d, pltpu.roll(x, d, axis=x.ndim - 1), 0.0)
        d *= 2
    o_ref[...] = -x


def neg_cumsum(x, nseq):
    r, total = x.shape
    seg = total // nseq
    return pl.pallas_call(
        _neg_cumsum_kernel,
        grid=(nseq,),
        in_specs=[pl.BlockSpec((r, seg), lambda b: (0, b))],
        out_specs=pl.BlockSpec((r, seg), lambda b: (0, b)),
        out_shape=jax.ShapeDtypeStruct((r, total), F32),
        compiler_params=_params(("arbitrary",)),
        name="neg_cumsum",
    )(x)


def _key_bias_kernel(nc_ref, place_ref, o_ref):
    x = nc_ref[...] * LOG2E
    hi = x.astype(BF16)
    rest = x - hi.astype(F32)
    mid = rest.astype(BF16)
    lo = (rest - mid.astype(F32)).astype(BF16)
    parts = jnp.concatenate([hi, mid, lo], axis=0)
    tiles = lax.dot_general(parts, place_ref[...], (((0,), (0,)), ((), ())), preferred_element_type=F32)
    o_ref[0] = tiles.astype(BF16)


def key_bias(nc, b, s, ts):
    part = jnp.arange(BIAS_PARTS * LOGF_ROWS) // LOGF_ROWS
    head = jnp.arange(BIAS_PARTS * LOGF_ROWS) % LOGF_ROWS
    lane = jnp.arange(MIXER_WIDTH)
    place = ((lane[None, :] == (head * CHUNK_HEAD + part)[:, None]) & (head < FOX_HEADS)[:, None]).astype(BF16)
    return pl.pallas_call(
        _key_bias_kernel,
        grid=(b, s // ts),
        in_specs=[pl.BlockSpec((LOGF_ROWS, ts), lambda bi, i: (0, bi * (s // ts) + i)),
                  pl.BlockSpec((BIAS_PARTS * LOGF_ROWS, MIXER_WIDTH), lambda bi, i: (0, 0))],
        out_specs=pl.BlockSpec((1, ts, MIXER_WIDTH), lambda bi, i: (bi, i, 0)),
        out_shape=jax.ShapeDtypeStruct((b, s, MIXER_WIDTH), BF16),
        compiler_params=_params(("arbitrary", "arbitrary")),
        name="key_bias",
    )(nc, place)


def _fox_attn_kernel(qt_ref, k_ref, kb_ref, vt_ref, o_ref, qa_ref, sa_ref, sb_ref, m_ref, l_ref, acc_ref,
                     *, blk, hp):
    i = pl.program_id(2)
    heads = [slice(hh * CHUNK_HEAD, (hh + 1) * CHUNK_HEAD) for hh in range(hp)]

    ones = (lax.broadcasted_iota(jnp.int32, (CHUNK_HEAD, blk), 0) < BIAS_PARTS).astype(BF16)
    for hh, sl in enumerate(heads):
        qa_ref[hh, 0:CHUNK_HEAD, :] = qt_ref[sl, :]
        qa_ref[hh, CHUNK_HEAD:2 * CHUNK_HEAD, :] = ones
    m_ref[...] = jnp.full(m_ref.shape, -jnp.inf, F32)
    l_ref[...] = jnp.zeros(l_ref.shape, F32)
    acc_ref[...] = jnp.zeros(acc_ref.shape, F32)

    def scores(j, dst):
        start = pl.multiple_of(j * blk, blk)
        for hh, sl in enumerate(heads):
            ka = jnp.concatenate([k_ref[0, pl.ds(start, blk), sl], kb_ref[0, pl.ds(start, blk), sl]], axis=1)
            dst[hh] = _dot(ka, qa_ref[hh])

    def absorb(j, src, diagonal):
        start = pl.multiple_of(j * blk, blk)
        for hh, sl in enumerate(heads):
            s = src[hh]
            if diagonal:
                key = lax.broadcasted_iota(jnp.int32, s.shape, 0)
                qry = lax.broadcasted_iota(jnp.int32, s.shape, 1)
                s = jnp.where(key <= qry, s, -jnp.inf)
            m_prev = m_ref[hh]
            m_new = jnp.maximum(m_prev, jnp.max(s, axis=0, keepdims=True))
            alpha = jnp.exp2(m_prev - m_new)
            p = jnp.exp2(s - m_new)
            l_ref[hh] = alpha * l_ref[hh] + jnp.sum(p, axis=0, keepdims=True)
            acc_ref[hh] = alpha * acc_ref[hh] + _dot(vt_ref[sl, pl.ds(start, blk)], p.astype(BF16))
            m_ref[hh] = m_new

    scores(0, sa_ref)

    def pair(p, carry):
        j = 2 * p
        scores(j + 1, sb_ref)
        absorb(j, sa_ref, False)
        scores(j + 2, sa_ref)
        absorb(j + 1, sb_ref, False)
        return carry

    lax.fori_loop(0, jnp.right_shift(i, 1), pair, 0)

    @pl.when(jnp.bitwise_and(i, 1) == 0)
    def _():
        absorb(i, sa_ref, True)

    @pl.when(jnp.bitwise_and(i, 1) == 1)
    def _():
        scores(i, sb_ref)
        absorb(i - 1, sa_ref, False)
        absorb(i, sb_ref, True)

    for hh in range(hp):
        out_t = acc_ref[hh] * (1.0 / l_ref[hh])
        o_ref[0, :, hh * CHUNK_HEAD:(hh + 1) * CHUNK_HEAD] = out_t.T.astype(BF16)


def fox_attention(qt, k, kb, vt, blk, hp):
    b, s, w = k.shape
    h = w // CHUNK_HEAD
    wide = hp * CHUNK_HEAD
    nq = s // blk
    return pl.pallas_call(
        functools.partial(_fox_attn_kernel, blk=blk, hp=hp),
        grid=(b, h // hp, nq),
        in_specs=[pl.BlockSpec((wide, blk), lambda bi, hi, i: (hi, bi * nq + i)),
                  pl.BlockSpec((1, s, wide), lambda bi, hi, i: (bi, 0, hi)),
                  pl.BlockSpec((1, s, wide), lambda bi, hi, i: (bi, 0, hi)),
                  pl.BlockSpec((wide, s), lambda bi, hi, i: (hi, bi))],
        out_specs=pl.BlockSpec((1, blk, wide), lambda bi, hi, i: (bi, i, hi)),
        out_shape=jax.ShapeDtypeStruct((b, s, w), BF16),
        scratch_shapes=[pltpu.VMEM((hp, 2 * CHUNK_HEAD, blk), BF16),
                        pltpu.VMEM((hp, blk, blk), F32), pltpu.VMEM((hp, blk, blk), F32),
                        pltpu.VMEM((hp, 1, blk), F32), pltpu.VMEM((hp, 1, blk), F32),
                        pltpu.VMEM((hp, CHUNK_HEAD, blk), F32)],
        compiler_params=_params(("arbitrary", "arbitrary", "arbitrary")),
        name="fox_attention",
    )(qt, k, kb, vt)


def _fox_decode_kernel(q_ref, kc_ref, vc_ref, kn_ref, vn_ref, nc_ref, o_ref, *, past, t, heads):
    row = lax.broadcasted_iota(jnp.int32, (t, t), 0)
    col = lax.broadcasted_iota(jnp.int32, (t, t), 1)
    for h in range(heads):
        sl = slice(h * CHUNK_HEAD, (h + 1) * CHUNK_HEAD)
        q = q_ref[0, :, sl]
        s1 = _dot_nt(q, kc_ref[0, h].astype(BF16)) + nc_ref[h, :, 0:past]
        s2 = _dot_nt(q, kn_ref[0, :, sl]) + nc_ref[h, :, past:past + t]
        s2 = jnp.where(col <= row, s2, -jnp.inf)
        m = jnp.maximum(jnp.max(s1, axis=-1, keepdims=True), jnp.max(s2, axis=-1, keepdims=True))
        p1 = jnp.exp(s1 - m)
        p2 = jnp.exp(s2 - m)
        l = jnp.sum(p1, axis=-1, keepdims=True) + jnp.sum(p2, axis=-1, keepdims=True)
        o = _dot(p1.astype(BF16), vc_ref[0, h].astype(BF16)) + _dot(p2.astype(BF16), vn_ref[0, :, sl])
        o_ref[0, :, sl] = (o / l).astype(BF16)


def fox_decode_attention(q, kc, vc, kn, vn, nc):
    b, t, w = q.shape
    _, h, past, dh = kc.shape
    npad = nc.shape[-1]
    new = pl.BlockSpec((1, t, w), lambda bi: (bi, 0, 0))
    old = pl.BlockSpec((1, h, past, dh), lambda bi: (bi, 0, 0, 0))
    return pl.pallas_call(
        functools.partial(_fox_decode_kernel, past=past, t=t, heads=h),
        grid=(b,),
        in_specs=[new, old, old, new, new,
                  pl.BlockSpec((LOGF_ROWS, 1, npad), lambda bi: (bi, 0, 0))],
        out_specs=new,
        out_shape=jax.ShapeDtypeStruct((b, t, w), BF16),
        compiler_params=_params(("arbitrary",)),
        name="fox_decode_attention",
    )(q, kc, vc, kn, vn, nc)


def _mem_attn_kernel(q_ref, mk_ref, mv_ref, o_ref):
    for h in range(XATTN_HEADS):
        sl = slice(h * CHUNK_HEAD, (h + 1) * CHUNK_HEAD)
        s = _dot_nt(q_ref[0, :, sl], mk_ref[0, :, sl]) * ATTN_SCALE
        p = jnp.exp(s - jnp.max(s, axis=-1, keepdims=True))
        l = jnp.sum(p, axis=-1, keepdims=True)
        o = _dot(p.astype(BF16), mv_ref[0, :, sl])
        o_ref[0, :, sl] = (o / l).astype(BF16)


def mem_attention(qx, mk, mv, tm):
    b, t, w = qx.shape
    nm = mk.shape[1]
    return pl.pallas_call(
        _mem_attn_kernel,
        grid=(b, t // tm),
        in_specs=[pl.BlockSpec((1, tm, w), lambda bi, i: (bi, i, 0)),
                  pl.BlockSpec((1, nm, w), lambda bi, i: (bi, 0, 0)),
                  pl.BlockSpec((1, nm, w), lambda bi, i: (bi, 0, 0))],
        out_specs=pl.BlockSpec((1, tm, w), lambda bi, i: (bi, i, 0)),
        out_shape=jax.ShapeDtypeStruct((b, t, w), BF16),
        compiler_params=_params(("arbitrary", "arbitrary")),
        name="mem_attention",
    )(qx, mk, mv)


def _out_proj_kernel(x_ref, mix_ref, xo_ref, w1_ref, w2_ref, o_ref):
    o_ref[...] = x_ref[...] + _dot(mix_ref[...], w1_ref[...]) + _dot(xo_ref[...], w2_ref[...])


def out_proj(x, mix, xo, w, layer, tm):
    t, d = x.shape
    c = mix.shape[1]
    return pl.pallas_call(
        _out_proj_kernel,
        grid=(t // tm,),
        in_specs=[pl.BlockSpec((tm, d), lambda i: (i, 0)),
                  pl.BlockSpec((tm, c), lambda i: (i, 0)),
                  pl.BlockSpec((tm, XATTN_WIDTH), lambda i: (i, 0)),
                  pl.BlockSpec((None, c, d), lambda i: (layer, 0, 0)),
                  pl.BlockSpec((None, XATTN_WIDTH, d), lambda i: (layer, c // XATTN_WIDTH, 0))],
        out_specs=pl.BlockSpec((tm, d), lambda i: (i, 0)),
        out_shape=jax.ShapeDtypeStruct((t, d), F32),
        compiler_params=_params(("arbitrary",)),
        name="out_proj",
    )(x, mix, xo, w, w)


def _mlp_kernel(x_ref, g_ref, wu_ref, wd_ref, gf_ref, o_ref, xn_ref, *, final_norm):
    f = pl.program_id(1)

    @pl.when(f == 0)
    def _():
        x = x_ref[...]
        xn_ref[...] = _rms(x, g_ref[...]).astype(BF16)
        o_ref[...] = x

    h = jnp.maximum(_dot(xn_ref[...], wu_ref[...]), 0.0)
    o_ref[...] += _dot((h * h).astype(BF16), wd_ref[...])

    if final_norm:
        @pl.when(f == pl.num_programs(1) - 1)
        def _():
            o_ref[...] = _rms(o_ref[...], gf_ref[...])


def mlp(x, g, w_up, w_down, g_final, layer, tm, tf, final_norm):
    t, d = x.shape
    ff = w_up.shape[2]
    return pl.pallas_call(
        functools.partial(_mlp_kernel, final_norm=final_norm),
        grid=(t // tm, ff // tf),
        in_specs=[pl.BlockSpec((tm, d), lambda i, f: (i, 0)),
                  pl.BlockSpec((1, d), lambda i, f: (0, 0)),
                  pl.BlockSpec((None, d, tf), lambda i, f: (layer, 0, f)),
                  pl.BlockSpec((None, tf, d), lambda i, f: (layer, f, 0)),
                  pl.BlockSpec((1, d), lambda i, f: (0, 0))],
        out_specs=pl.BlockSpec((tm, d), lambda i, f: (i, 0)),
        out_shape=jax.ShapeDtypeStruct((t, d), F32),
        scratch_shapes=[pltpu.VMEM((tm, d), BF16)],
        compiler_params=_params(("arbitrary", "arbitrary")),
        name="mlp",
    )(x, g, w_up, w_down, g_final)


def _token_tile(t, pref):
    return pref if t % pref == 0 else t


def kernel(x_prompt, x_sample, cache_mem_k, cache_mem_v, state_conv, cache_fox_k, cache_fox_v, cache_fox_logf, mem_prompt, g_mix, g_mem, w_mem_k, w_mem_v, w_in_conv, conv_w, conv_b, conv_ln_g, conv_ln_b, w_in_fox, b_fox_f, w_out, g_mlp, w_up, w_down, g_final):
    bp, sp, d = x_prompt.shape
    bs, ss, _ = x_sample.shape
    depth = g_mix.shape[0]
    n_mem = mem_prompt.shape[1]
    past = cache_fox_k.shape[2]
    c = MIXER_WIDTH

    groups = {
        "p": dict(y=x_prompt.reshape(bp * sp, d), b=bp, s=sp, tm_conv=512, tm_fox=1024),
        "s": dict(y=x_sample.reshape(bs * ss, d), b=bs, s=ss, tm_conv=bs * ss, tm_fox=bs * ss),
    }
    mem_flat = mem_prompt.reshape(bp * n_mem, d)
    row = lambda a: a.reshape(1, -1)
    w_o, w_u, w_d = w_out.astype(BF16), w_up.astype(BF16), w_down.astype(BF16)

    mem_k_p, mem_v_p, conv_p, conv_s = [], [], [], []
    fk, fv, fl = {"p": [], "s": []}, {"p": [], "s": []}, {"p": [], "s": []}

    for i in range(depth):
        j = i // 2
        w_kv = jnp.concatenate([w_mem_k[i], w_mem_v[i]], axis=1).astype(BF16)
        kv = rms_matmul(mem_flat, row(g_mem[i]), w_kv, XATTN_WIDTH)
        mk_p = kv[:, :XATTN_WIDTH].reshape(bp, n_mem, XATTN_WIDTH)
        mv_p = kv[:, XATTN_WIDTH:].reshape(bp, n_mem, XATTN_WIDTH)
        mem_k_p.append(mk_p.reshape(bp, n_mem, XATTN_HEADS, CHUNK_HEAD))
        mem_v_p.append(mv_p.reshape(bp, n_mem, XATTN_HEADS, CHUNK_HEAD))
        mem = {"p": (mk_p.astype(BF16), mv_p.astype(BF16)),
               "s": (cache_mem_k[i].reshape(bs, n_mem, XATTN_WIDTH).astype(BF16),
                     cache_mem_v[i].reshape(bs, n_mem, XATTN_WIDTH).astype(BF16))}
        if i % 2 == 0:
            w_in = w_in_conv[j].astype(BF16)
        else:
            wf = w_in_fox[j]
            w_in = wf.T.astype(BF16)
            w_qx = w_in[3 * c + FOX_HEADS:]
            wf_t = jnp.pad(w_in[3 * c:3 * c + FOX_HEADS], ((0, LOGF_ROWS - FOX_HEADS), (0, 0)))
            bf = jnp.zeros((LOGF_ROWS, 1), F32).at[:FOX_HEADS, 0].set(b_fox_f[j])

        for name, grp in groups.items():
            y, b, s = grp["y"], grp["b"], grp["s"]
            if i % 2 == 0:
                if name == "p":
                    hist = jnp.zeros((b, CONV_HIST, c), F32)
                else:
                    hist = jnp.concatenate([jnp.zeros((b, CONV_HIST - CONV_STATE, c), F32), state_conv[j]], axis=1)
                mix, qx, tail = conv_front(y, row(g_mix[i]), w_in, hist, conv_w[j], row(conv_b[j]),
                                           row(conv_ln_g[j]), row(conv_ln_b[j]), grp["tm_conv"], XATTN_WIDTH)
                (conv_p if name == "p" else conv_s).append(tail[:, CONV_HIST - CONV_STATE:])
            else:
                q, k_f, k_b, v_f, v_b, qx, lf_t = fox_inproj(
                    y, row(g_mix[i]), w_in, w_qx, wf_t, bf, b, grp["tm_fox"], XATTN_WIDTH, transposed=(name == "p"),
                    q_scale=ATTN_SCALE * LOG2E if name == "p" else ATTN_SCALE)
                fk[name].append(k_f.transpose(0, 2, 1, 3))
                fv[name].append(v_f.transpose(0, 2, 1, 3))
                fl[name].append(lf_t[:FOX_HEADS].reshape(FOX_HEADS, b, s).transpose(1, 2, 0))
                k_b = k_b.reshape(b, s, c)
                if name == "p":
                    nc = neg_cumsum(lf_t, b)
                    mix = fox_attention(q, k_b, key_bias(nc, b, s, 512), v_b, 512, 3)
                else:
                    q = q.reshape(b, s, c)
                    v_b = v_b.reshape(b, s, c)
                    total = past + s
                    padded = -(-total // LANE) * LANE
                    lf_old = jnp.zeros((b, LOGF_ROWS, past), F32).at[:, :FOX_HEADS].set(
                        cache_fox_logf[j].astype(F32).transpose(0, 2, 1))
                    lf_all = jnp.concatenate(
                        [lf_old, lf_t.reshape(LOGF_ROWS, b, s).transpose(1, 0, 2),
                         jnp.zeros((b, LOGF_ROWS, padded - total), F32)], axis=2)
                    lf_all = lf_all.transpose(1, 0, 2).reshape(LOGF_ROWS, b * padded)
                    nc = neg_cumsum(lf_all, b)
                    nc = nc.reshape(LOGF_ROWS, b, padded).transpose(1, 0, 2).reshape(b * LOGF_ROWS, 1, padded)
                    mix = fox_decode_attention(q, cache_fox_k[j].transpose(0, 2, 1, 3),
                                               cache_fox_v[j].transpose(0, 2, 1, 3), k_b, v_b, nc)
            qx = qx.reshape(b, s, XATTN_WIDTH)
            xo = mem_attention(qx, mem[name][0], mem[name][1], _token_tile(s, 1024))
            y = out_proj(y, mix.reshape(b * s, c), xo.reshape(b * s, XATTN_WIDTH), w_o, i, _token_tile(b * s, 512))
            y = mlp(y, row(g_mlp[i]), w_u, w_d, row(g_final), i, _token_tile(b * s, 512), 1024,
                    final_norm=(i == depth - 1))
            grp["y"] = y

    y_prompt = groups["p"]["y"].reshape(bp, sp, d)
    y_sample = groups["s"]["y"].reshape(bs, ss, d)
    return (y_prompt, y_sample, jnp.stack(mem_k_p), jnp.stack(mem_v_p), jnp.stack(conv_p), jnp.stack(conv_s),
            jnp.stack(fk["p"]), jnp.stack(fv["p"]), jnp.stack(fl["p"]),
            jnp.stack(fk["s"]), jnp.stack(fv["s"]), jnp.stack(fl["s"]))
```
